```python
import math
import jax, jax.numpy as jnp
from jax import lax
import numpy as np

D_MODEL = 2048
BATCH = 2
SEQ = 8192
DEPTH = 1

D_MIX = D_MODEL
CONV_CH = D_MIX // 2
CONV_GROUPS = 16
CONV_WIDTH = 31
N_HEADS = 16
HEAD_DIM = 64
ATT_WIDTH = N_HEADS * HEAD_DIM
D_IN = 2 * CONV_CH + 3 * ATT_WIDTH
D_FF = 5632
DILATED_PATTERNS = ((128, 1), (512, 4), (2048, 16))
ALIBI_MAX_BIAS = 8.0
EPS = 1e-6

kernel_name = "hymba_conformer_dilated_alibi_layer"


def rms_norm(x, g):
    xf = x.astype(jnp.float32)
    y = xf * lax.rsqrt(jnp.mean(xf * xf, axis=-1, keepdims=True) + EPS)
    return (y * g.astype(jnp.float32)).astype(x.dtype)


def layer_norm(x, g, b):
    xf = x.astype(jnp.float32)
    mu = jnp.mean(xf, axis=-1, keepdims=True)
    var = jnp.mean(jnp.square(xf - mu), axis=-1, keepdims=True)
    y = (xf - mu) * lax.rsqrt(var + EPS)
    return (y * g.astype(jnp.float32) + b.astype(jnp.float32)).astype(x.dtype)


def swiglu_ffn(x, w_gate, w_up, w_down):
    return (jax.nn.silu(x @ w_gate) * (x @ w_up)) @ w_down


def alibi_slopes(n_heads):
    return 2.0 ** (-ALIBI_MAX_BIAS * jnp.arange(1, n_heads + 1, dtype=jnp.float32) / n_heads)


def conformer_conv(u, w_dw, b_dw, ln_g, ln_b):
    a, gate = jnp.split(u, 2, axis=-1)
    v = a * jax.nn.sigmoid(gate)
    v_pad = jnp.pad(v, ((0, 0), (CONV_WIDTH - 1, 0), (0, 0)))
    y = lax.conv_general_dilated(
        v_pad, w_dw[:, None, :].astype(v.dtype), window_strides=(1,), padding="VALID",
        dimension_numbers=("NWC", "WIO", "NWC"), feature_group_count=CONV_CH)
    y = y + b_dw.astype(y.dtype)
    y = layer_norm(y, ln_g, ln_b)
    return jax.nn.silu(y)


def dilated_branch(q, k, v, window, dilation, slopes):
    b, h, s, hd = q.shape
    w = window // dilation
    chunk = dilation * w
    s_pad = -(-s // chunk) * chunk
    n_sub = s_pad // dilation
    nb = n_sub // w

    def to_blocks(t):
        t = jnp.pad(t, ((0, 0), (0, 0), (0, s_pad - s), (0, 0)))
        t = t.reshape(b, h, n_sub, dilation, hd).transpose(0, 1, 3, 2, 4)
        return t.reshape(b, h, dilation, nb, w, hd)

    def with_prev(t):
        prev = jnp.pad(t[:, :, :, :-1], ((0, 0), (0, 0), (0, 0), (1, 0), (0, 0), (0, 0)))
        return jnp.concatenate([prev, t], axis=4)

    qb = to_blocks(q)
    kk = with_prev(to_blocks(k))
    vv = with_prev(to_blocks(v))

    scores = jnp.einsum("bhrnqd,bhrnkd->bhrnqk", qb, kk).astype(jnp.float32)
    scores = scores * (1.0 / math.sqrt(hd))
    qi = jnp.arange(w)[:, None]
    kj = jnp.arange(2 * w)[None, :]
    sub_dist = w + qi - kj
    key_sub_pos = (jnp.arange(nb)[:, None, None] - 1) * w + kj[None]
    valid = (sub_dist >= 0)[None] & (sub_dist <= w)[None] & (key_sub_pos >= 0)
    token_dist = (dilation * sub_dist).astype(jnp.float32)
    bias = -slopes[:, None, None] * token_dist[None]
    scores = scores + bias[None, :, None, None]
    scores = jnp.where(valid[None, None, None], scores, -jnp.inf)

    lse = jax.nn.logsumexp(scores, axis=-1)
    p = jnp.exp(scores - lse[..., None])
    out = jnp.einsum("bhrnqk,bhrnkd->bhrnqd", p, vv.astype(jnp.float32))

    out = out.reshape(b, h, dilation, n_sub, hd).transpose(0, 1, 3, 2, 4).reshape(b, h, s_pad, hd)
    lse = lse.reshape(b, h, dilation, n_sub).transpose(0, 1, 3, 2).reshape(b, h, s_pad)
    return out[:, :, :s], lse[:, :, :s]


def dilated_attention(zq, zk, zv, q_norm_g, k_norm_g):
    b, s, _ = zq.shape
    heads = lambda t: t.reshape(b, s, N_HEADS, HEAD_DIM).transpose(0, 2, 1, 3)
    q = rms_norm(heads(zq), q_norm_g)
    k = rms_norm(heads(zk), k_norm_g)
    v = heads(zv)
    slopes = alibi_slopes(N_HEADS)
    outs, lses = [], []
    for window, dilation in DILATED_PATTERNS:
        o, l = dilated_branch(q, k, v, window, dilation, slopes)
        outs.append(o)
        lses.append(l)
    wts = jax.nn.softmax(jnp.stack(lses, axis=0), axis=0)
    out = jnp.sum(wts[..., None] * jnp.stack(outs, axis=0), axis=0)
    return out.transpose(0, 2, 1, 3).reshape(b, s, ATT_WIDTH).astype(zq.dtype)


def setup_inputs(seed: int = 0) -> dict:
    key = jax.random.key(seed)
    ks = jax.random.split(key, 20)
    f32 = jnp.float32
    nrm = lambda k, shape, fan_in: jax.random.normal(k, shape, f32) * (fan_in ** -0.5)
    gain = lambda k, shape: 1.0 + 0.02 * jax.random.normal(k, shape, f32)
    small = lambda k, shape: 0.02 * jax.random.normal(k, shape, f32)
    L = DEPTH
    return {
        "x": jax.random.normal(ks[0], (BATCH, SEQ, D_MODEL), f32),
        "ffn1_norm_g": gain(ks[1], (L, D_MODEL)),
        "ffn1_w_gate": nrm(ks[2], (L, D_MODEL, D_FF), D_MODEL),
        "ffn1_w_up": nrm(ks[3], (L, D_MODEL, D_FF), D_MODEL),
        "ffn1_w_down": nrm(ks[4], (L, D_FF, D_MODEL), D_FF),
        "mix_norm_g": gain(ks[5], (L, D_MODEL)),
        "w_in": nrm(ks[6], (L, D_MODEL, D_IN), D_MODEL),
        "conv_w_dw": nrm(ks[7], (L, CONV_WIDTH, CONV_CH), CONV_WIDTH),
        "conv_b_dw": small(ks[8], (L, CONV_CH)),
        "conv_ln_g": gain(ks[9], (L, CONV_CH)),
        "conv_ln_b": small(ks[10], (L, CONV_CH)),
        "q_norm_g": gain(ks[11], (L, HEAD_DIM)),
        "k_norm_g": gain(ks[12], (L, HEAD_DIM)),
        "w_out": nrm(ks[13], (L, D_MIX, D_MODEL), D_MIX),
        "ffn2_norm_g": gain(ks[14], (L, D_MODEL)),
        "ffn2_w_gate": nrm(ks[15], (L, D_MODEL, D_FF), D_MODEL),
        "ffn2_w_up": nrm(ks[16], (L, D_MODEL, D_FF), D_MODEL),
        "ffn2_w_down": nrm(ks[17], (L, D_FF, D_MODEL), D_FF),
    }


def reference(x, ffn1_norm_g, ffn1_w_gate, ffn1_w_up, ffn1_w_down, mix_norm_g, w_in,
              conv_w_dw, conv_b_dw, conv_ln_g, conv_ln_b, q_norm_g, k_norm_g, w_out,
              ffn2_norm_g, ffn2_w_gate, ffn2_w_up, ffn2_w_down):
    for l in range(DEPTH):
        x = x + 0.5 * swiglu_ffn(rms_norm(x, ffn1_norm_g[l]), ffn1_w_gate[l], ffn1_w_up[l], ffn1_w_down[l])
        h = rms_norm(x, mix_norm_g[l])
        z = h @ w_in[l]
        c0 = 2 * CONV_CH
        z_conv = z[..., :c0]
        z_q = z[..., c0:c0 + ATT_WIDTH]
        z_k = z[..., c0 + ATT_WIDTH:c0 + 2 * ATT_WIDTH]
        z_v = z[..., c0 + 2 * ATT_WIDTH:]
        y_conv = conformer_conv(z_conv, conv_w_dw[l], conv_b_dw[l], conv_ln_g[l], conv_ln_b[l])
        y_att = dilated_attention(z_q, z_k, z_v, q_norm_g[l], k_norm_g[l])
        y = jnp.concatenate([y_conv, y_att], axis=-1) @ w_out[l]
        x = x + y
        x = x + 0.5 * swiglu_ffn(rms_norm(x, ffn2_norm_g[l]), ffn2_w_gate[l], ffn2_w_up[l], ffn2_w_down[l])
    return x
```

```python
import jax
import jax.numpy as jnp
from jax import lax
from jax.experimental import pallas as pl
from jax.experimental.pallas import tpu as pltpu

D_MODEL = 2048
D_FF = 5632
CONV_CH = 1024
CONV_WIDTH = 31
N_HEADS = 16
HEAD_DIM = 64
ATT_WIDTH = N_HEADS * HEAD_DIM
DILATIONS = (1, 4, 16)
WIN = 128
ALIBI_MAX_BIAS = 8.0
EPS = 1e-6

LANES = 128
SUBLANES = 8
VMEM_LIMIT = 52 * 1024 * 1024

F32 = jnp.float32
BF16 = jnp.bfloat16
NEG = -1e30


def _rms_norm_rows(x, g):
    ms = jnp.mean(x * x, axis=-1, keepdims=True)
    return x * lax.rsqrt(ms + EPS) * g


FFN_TM = 512
FFN_TF = 512


def _ffn_kernel(x_ref, g_ref, wg_ref, wu_ref, wd_ref, o_ref, h_ref):
    j = pl.program_id(1)

    @pl.when(j == 0)
    def _():
        x = x_ref[...]
        h_ref[...] = _rms_norm_rows(x, g_ref[...]).astype(BF16)
        o_ref[...] = x

    h = h_ref[...]
    gate = jnp.dot(h, wg_ref[...], preferred_element_type=F32)
    up = jnp.dot(h, wu_ref[...], preferred_element_type=F32)
    a = (gate * jax.nn.sigmoid(gate) * (0.5 * up)).astype(BF16)
    o_ref[...] += jnp.dot(a, wd_ref[...], preferred_element_type=F32)


def _ffn(x, g, wg, wu, wd):
    n = x.shape[0]
    return pl.pallas_call(
        _ffn_kernel,
        grid=(n // FFN_TM, D_FF // FFN_TF),
        in_specs=[
            pl.BlockSpec((FFN_TM, D_MODEL), lambda i, j: (i, 0)),
            pl.BlockSpec((1, D_MODEL), lambda i, j: (0, 0)),
            pl.BlockSpec((D_MODEL, FFN_TF), lambda i, j: (0, j)),
            pl.BlockSpec((D_MODEL, FFN_TF), lambda i, j: (0, j)),
            pl.BlockSpec((FFN_TF, D_MODEL), lambda i, j: (j, 0)),
        ],
        out_specs=pl.BlockSpec((FFN_TM, D_MODEL), lambda i, j: (i, 0)),
        out_shape=jax.ShapeDtypeStruct((n, D_MODEL), F32),
        scratch_shapes=[pltpu.VMEM((FFN_TM, D_MODEL), BF16)],
        compiler_params=pltpu.CompilerParams(
            dimension_semantics=("parallel", "arbitrary"),
            vmem_limit_bytes=VMEM_LIMIT),
        name="ffn",
    )(x, g, wg, wu, wd)


INP_TM = 512
INP_TN = 1024
GLU_HALF = INP_TN // 2
HEAD_GROUP = 256


def _head_norm(z, gain, e):
    ss = z * z
    hi = ss.astype(BF16)
    lo = (ss - hi.astype(F32)).astype(BF16)
    outs = []
    for c in range(ATT_WIDTH // HEAD_GROUP):
        sl = slice(c * HEAD_GROUP, (c + 1) * HEAD_GROUP)
        tot = (jnp.dot(hi[:, sl], e, preferred_element_type=F32)
               + jnp.dot(lo[:, sl], e, preferred_element_type=F32))
        outs.append(z[:, sl] * lax.rsqrt(tot * (1.0 / HEAD_DIM) + EPS))
    return jnp.concatenate(outs, axis=1) * gain


def _inproj_kernel(x_ref, g_ref, w_ref, qg_ref, kg_ref, e_ref,
                   glu_ref, q_ref, k_ref, v_ref, h_ref):
    j = pl.program_id(1)

    @pl.when(j == 0)
    def _():
        h_ref[...] = _rms_norm_rows(x_ref[...], g_ref[...]).astype(BF16)

    z = jnp.dot(h_ref[...], w_ref[...], preferred_element_type=F32)

    @pl.when(j < 2)
    def _():
        glu_ref[...] = z[:, :GLU_HALF] * jax.nn.sigmoid(z[:, GLU_HALF:])

    @pl.when(j == 2)
    def _():
        q_ref[...] = _head_norm(z, qg_ref[...], e_ref[...]) * (HEAD_DIM ** -0.5)

    @pl.when(j == 3)
    def _():
        k_ref[...] = _head_norm(z, kg_ref[...], e_ref[...])

    @pl.when(j == 4)
    def _():
        v_ref[...] = z


def _inproj(x, g, w_perm, qg, kg, e):
    n = x.shape[0]
    att = jax.ShapeDtypeStruct((n, ATT_WIDTH), F32)
    att_spec = pl.BlockSpec((INP_TM, ATT_WIDTH), lambda i, j: (i, 0))
    const = lambda shape: pl.BlockSpec(shape, lambda i, j: (0, 0))
    return pl.pallas_call(
        _inproj_kernel,
        grid=(n // INP_TM, 5),
        in_specs=[
            pl.BlockSpec((INP_TM, D_MODEL), lambda i, j: (i, 0)),
            const((1, D_MODEL)),
            pl.BlockSpec((D_MODEL, INP_TN), lambda i, j: (0, j)),
            const((1, ATT_WIDTH)),
            const((1, ATT_WIDTH)),
            const((HEAD_GROUP, HEAD_GROUP)),
        ],
        out_specs=[
            pl.BlockSpec((INP_TM, GLU_HALF), lambda i, j: (i, jnp.minimum(j, 1))),
            att_spec, att_spec, att_spec,
        ],
        out_shape=[jax.ShapeDtypeStruct((n, CONV_CH), F32), att, att, att],
        scratch_shapes=[pltpu.VMEM((INP_TM, D_MODEL), BF16)],
        compiler_params=pltpu.CompilerParams(
            dimension_semantics=("parallel", "arbitrary"),
            vmem_limit_bytes=VMEM_LIMIT),
        name="in_proj",
    )(x, g, w_perm, qg, kg, e)


CONV_TS = 256
CONV_HALO = 32
CONV_RC = 32
CONV_PAD = CONV_HALO - (CONV_WIDTH - 1)
CONV_SH_ROWS = CONV_TS + CONV_HALO - SUBLANES


def _conv_kernel(cur_ref, prev_ref, w_ref, b_ref, g_ref, beta_ref, o_ref, buf_ref, sh_ref):
    i = pl.program_id(1)
    buf_ref[0:CONV_HALO, :] = jnp.where(i == 0, 0.0, prev_ref[0])
    buf_ref[CONV_HALO:, :] = cur_ref[0]
    for s in range(1, SUBLANES):
        sh_ref[s - 1] = buf_ref[s:s + CONV_SH_ROWS, :]

    bias = b_ref[...]
    gain = g_ref[...]
    beta = beta_ref[...]

    def chunk(c, carry):
        r0 = pl.multiple_of(c * CONV_RC, CONV_RC)
        acc = jnp.broadcast_to(bias, (CONV_RC, CONV_CH))
        for t in range(CONV_WIDTH):
            off = CONV_PAD + t
            phase, base = off % SUBLANES, off - off % SUBLANES
            if phase == 0:
                rows = buf_ref[pl.ds(r0 + base, CONV_RC), :]
            else:
                rows = sh_ref[phase - 1, pl.ds(r0 + base, CONV_RC), :]
            acc = acc + rows * w_ref[t:t + 1, :]
        mu = jnp.mean(acc, axis=-1, keepdims=True)
        cen = acc - mu
        var = jnp.mean(cen * cen, axis=-1, keepdims=True)
        y = cen * lax.rsqrt(var + EPS) * gain + beta
        o_ref[0, pl.ds(r0, CONV_RC), :] = (y * jax.nn.sigmoid(y)).astype(o_ref.dtype)
        return carry

    lax.fori_loop(0, CONV_TS // CONV_RC, chunk, 0)


def _conv(v, w, b, g, beta):
    bsz, s, _ = v.shape
    halo_blocks = CONV_TS // CONV_HALO
    const = lambda shape: pl.BlockSpec(shape, lambda bb, i: (0, 0))
    return pl.pallas_call(
        _conv_kernel,
        grid=(bsz, s // CONV_TS),
        in_specs=[
            pl.BlockSpec((1, CONV_TS, CONV_CH), lambda bb, i: (bb, i, 0)),
            pl.BlockSpec((1, CONV_HALO, CONV_CH),
                         lambda bb, i: (bb, jnp.maximum(i * halo_blocks - 1, 0), 0)),
            const((CONV_WIDTH, CONV_CH)),
            const((1, CONV_CH)), const((1, CONV_CH)), const((1, CONV_CH)),
        ],
        out_specs=pl.BlockSpec((1, CONV_TS, CONV_CH), lambda bb, i: (bb, i, 0)),
        out_shape=jax.ShapeDtypeStruct((bsz, s, CONV_CH), BF16),
        scratch_shapes=[
            pltpu.VMEM((CONV_TS + CONV_HALO, CONV_CH), F32),
            pltpu.VMEM((SUBLANES - 1, CONV_SH_ROWS, CONV_CH), F32),
        ],
        compiler_params=pltpu.CompilerParams(
            dimension_semantics=("parallel", "arbitrary"),
            vmem_limit_bytes=VMEM_LIMIT),
        name="conv",
    )(v, v, w, b, g, beta)


ATT_CH = 2048
ATT_QB = WIN
ATT_KB = 2 * WIN
ATT_TILES = ATT_CH // ATT_QB


def _attn_kernel(slopes_ref, q_ref, kc_ref, kp_ref, vc_ref, vp_ref, o_ref,
                 kbuf, vbuf, obuf, mbuf, rbuf):
    c = pl.program_id(1)
    hp = pl.program_id(2)
    first_chunk = c == 0

    kbuf[0:ATT_CH, :] = kp_ref[0]
    kbuf[ATT_CH:, :] = kc_ref[0]
    vbuf[0:ATT_CH, :] = vp_ref[0]
    vbuf[ATT_CH:, :] = vc_ref[0]

    lane = lax.broadcasted_iota(jnp.int32, (1, LANES), 1)
    qi = lax.broadcasted_iota(jnp.int32, (ATT_QB, ATT_KB), 0)
    kj = lax.broadcasted_iota(jnp.int32, (ATT_QB, ATT_KB), 1)
    dist = WIN + qi - kj
    valid = (dist >= 0) & (dist <= WIN)
    valid_first = valid & (kj >= WIN)
    distf = dist.astype(F32)

    for hh in range(2):
        head_lanes = (lane < HEAD_DIM) if hh == 0 else (lane >= HEAD_DIM)
        slope = slopes_ref[2 * hp + hh]
        for bi, d in enumerate(DILATIONS):
            alibi = (-slope * d) * distf
            bias = jnp.where(valid, alibi, NEG)
            bias_first = jnp.where(valid_first, alibi, NEG)
            span = d * WIN

            def tile(t, carry, d=d, bi=bi, bias=bias, bias_first=bias_first, span=span,
                     head_lanes=head_lanes):
                blk = t // d
                res = t - blk * d
                q_start = blk * span + res
                k_start = ATT_CH + q_start - span
                first = jnp.logical_and(first_chunk, blk == 0)
                q = q_ref[0, pl.ds(q_start, ATT_QB, stride=d), :]
                k = kbuf[pl.ds(k_start, ATT_KB, stride=d), :]
                v = vbuf[pl.ds(k_start, ATT_KB, stride=d), :]
                qm = jnp.where(head_lanes, q, 0.0).astype(BF16)
                s = lax.dot_general(qm, k.astype(BF16), (((1,), (1,)), ((), ())),
                                    preferred_element_type=F32)
                s = s + jnp.where(first, bias_first, bias)
                m = jnp.max(s, axis=-1, keepdims=True)
                p = jnp.exp(s - m)
                vext = jnp.where(head_lanes, v, 1.0).astype(BF16)
                oe = jnp.dot(p.astype(BF16), vext, preferred_element_type=F32)
                obuf[bi, pl.ds(q_start, ATT_QB, stride=d), :] = oe
                mbuf[bi, pl.ds(q_start, ATT_QB, stride=d), :] = jnp.broadcast_to(m, (ATT_QB, LANES))
                return carry

            lax.fori_loop(0, ATT_TILES, tile, 0)

        def combine(t, carry, hh=hh):
            rows = pl.ds(pl.multiple_of(t * ATT_QB, ATT_QB), ATT_QB)
            m0, m1, m2 = mbuf[0, rows, :], mbuf[1, rows, :], mbuf[2, rows, :]
            mm = jnp.maximum(jnp.maximum(m0, m1), m2)
            tot = (jnp.exp(m0 - mm) * obuf[0, rows, :] + jnp.exp(m1 - mm) * obuf[1, rows, :]
                   + jnp.exp(m2 - mm) * obuf[2, rows, :])
            res = tot / pltpu.roll(tot, HEAD_DIM, axis=1)
            if hh == 0:
                rbuf[rows, :] = res
            else:
                o_ref[0, rows, :] = jnp.where(lane < HEAD_DIM, rbuf[rows, :], res).astype(o_ref.dtype)
            return carry

        lax.fori_loop(0, ATT_TILES, combine, 0)


def _attn(q, k, v, slopes):
    bsz, s, _ = q.shape
    blk = (1, ATT_CH, LANES)
    cur = pl.BlockSpec(blk, lambda b, c, h, sl: (b, c, h))
    prev = pl.BlockSpec(blk, lambda b, c, h, sl: (b, jnp.maximum(c - 1, 0), h))
    grid_spec = pltpu.PrefetchScalarGridSpec(
        num_scalar_prefetch=1,
        grid=(bsz, s // ATT_CH, ATT_WIDTH // LANES),
        in_specs=[cur, cur, prev, cur, prev],
        out_specs=cur,
        scratch_shapes=[
            pltpu.VMEM((2 * ATT_CH, LANES), F32),
            pltpu.VMEM((2 * ATT_CH, LANES), F32),
            pltpu.VMEM((len(DILATIONS), ATT_CH, LANES), F32),
            pltpu.VMEM((len(DILATIONS), ATT_CH, LANES), F32),
            pltpu.VMEM((ATT_CH, LANES), F32),
        ],
    )
    return pl.pallas_call(
        _attn_kernel,
        grid_spec=grid_spec,
        out_shape=jax.ShapeDtypeStruct((bsz, s, ATT_WIDTH), BF16),
        compiler_params=pltpu.CompilerParams(
            dimension_semantics=("parallel", "arbitrary", "arbitrary"),
            vmem_limit_bytes=VMEM_LIMIT),
        name="attn",
    )(slopes, q, k, k, v, v)


OUT_TM = 512


def _outproj_kernel(x_ref, yc_ref, ya_ref, wc_ref, wa_ref, o_ref):
    o_ref[...] = (x_ref[...]
                  + jnp.dot(yc_ref[...], wc_ref[...], preferred_element_type=F32)
                  + jnp.dot(ya_ref[...], wa_ref[...], preferred_element_type=F32))


def _outproj(x, yc, ya, wc, wa):
    n = x.shape[0]
    return pl.pallas_call(
        _outproj_kernel,
        grid=(n // OUT_TM,),
        in_specs=[
            pl.BlockSpec((OUT_TM, D_MODEL), lambda i: (i, 0)),
            pl.BlockSpec((OUT_TM, CONV_CH), lambda i: (i, 0)),
            pl.BlockSpec((OUT_TM, ATT_WIDTH), lambda i: (i, 0)),
            pl.BlockSpec((CONV_CH, D_MODEL), lambda i: (0, 0)),
            pl.BlockSpec((ATT_WIDTH, D_MODEL), lambda i: (0, 0)),
        ],
        out_specs=pl.BlockSpec((OUT_TM, D_MODEL), lambda i: (i, 0)),
        out_shape=jax.ShapeDtypeStruct((n, D_MODEL), F32),
        compiler_params=pltpu.CompilerParams(
            dimension_semantics=("parallel",),
            vmem_limit_bytes=VMEM_LIMIT),
        name="out_proj",
    )(x, yc, ya, wc, wa)


def _layer(x2d, bsz, s, p):
    n = bsz * s
    x2d = _ffn(x2d, p["ffn1_norm_g"], p["ffn1_w_gate"], p["ffn1_w_up"], p["ffn1_w_down"])
    glu, q, k, v = _inproj(x2d, p["mix_norm_g"], p["w_in"], p["q_norm_g"], p["k_norm_g"], p["head_ones"])
    yc = _conv(glu.reshape(bsz, s, CONV_CH), p["conv_w_dw"], p["conv_b_dw"], p["conv_ln_g"], p["conv_ln_b"])
    ya = _attn(q.reshape(bsz, s, ATT_WIDTH), k.reshape(bsz, s, ATT_WIDTH), v.reshape(bsz, s, ATT_WIDTH),
               p["slopes"])
    x2d = _outproj(x2d, yc.reshape(n, CONV_CH), ya.reshape(n, ATT_WIDTH), p["w_out_conv"], p["w_out_att"])
    return _ffn(x2d, p["ffn2_norm_g"], p["ffn2_w_gate"], p["ffn2_w_up"], p["ffn2_w_down"])


def kernel(x, ffn1_norm_g, ffn1_w_gate, ffn1_w_up, ffn1_w_down, mix_norm_g, w_in, conv_w_dw, conv_b_dw,
           conv_ln_g, conv_ln_b, q_norm_g, k_norm_g, w_out, ffn2_norm_g, ffn2_w_gate, ffn2_w_up, ffn2_w_down):
    bsz, s, _ = x.shape
    depth = w_in.shape[0]
    row = lambda a: a.reshape(1, -1).astype(F32)
    head_ids = jnp.arange(HEAD_GROUP) // HEAD_DIM
    head_ones = (head_ids[:, None] == head_ids[None, :]).astype(BF16)
    slopes = 2.0 ** (-ALIBI_MAX_BIAS * jnp.arange(1, N_HEADS + 1, dtype=F32) / N_HEADS)
    x2d = x.reshape(bsz * s, D_MODEL)
    for l in range(depth):
        w = w_in[l]
        wa, wg = w[:, :CONV_CH], w[:, CONV_CH:2 * CONV_CH]
        w_perm = jnp.concatenate(
            [wa[:, :GLU_HALF], wg[:, :GLU_HALF], wa[:, GLU_HALF:], wg[:, GLU_HALF:], w[:, 2 * CONV_CH:]],
            axis=1).astype(BF16)
        p = dict(
            ffn1_norm_g=row(ffn1_norm_g[l]), ffn1_w_gate=ffn1_w_gate[l].astype(BF16),
            ffn1_w_up=ffn1_w_up[l].astype(BF16), ffn1_w_down=ffn1_w_down[l].astype(BF16),
            mix_norm_g=row(mix_norm_g[l]), w_in=w_perm,
            q_norm_g=row(jnp.tile(q_norm_g[l], N_HEADS)), k_norm_g=row(jnp.tile(k_norm_g[l], N_HEADS)),
            head_ones=head_ones, slopes=slopes,
            conv_w_dw=conv_w_dw[l].astype(F32), conv_b_dw=row(conv_b_dw[l]),
            conv_ln_g=row(conv_ln_g[l]), conv_ln_b=row(conv_ln_b[l]),
            w_out_conv=w_out[l, :CONV_CH].astype(BF16), w_out_att=w_out[l, CONV_CH:].astype(BF16),
            ffn2_norm_g=row(ffn2_norm_g[l]), ffn2_w_gate=ffn2_w_gate[l].astype(BF16),
            ffn2_w_up=ffn2_w_up[l].astype(BF16), ffn2_w_down=ffn2_w_down[l].astype(BF16),
        )
        x2d = _layer(x2d, bsz, s, p)
    return x2d.reshape(bsz, s, D_MODEL)
```

```python
import jax
import jax.numpy as jnp
from jax import lax
from jax.experimental import pallas as pl
from jax.experimental.pallas import tpu as pltpu

D_MODEL = 2048
D_FF = 5632
CONV_CH = 1024
CONV_WIDTH = 31
N_HEADS = 16
HEAD_DIM = 64
ATT_WIDTH = N_HEADS * HEAD_DIM
DILATIONS = (1, 4, 16)
WIN = 128
ALIBI_MAX_BIAS = 8.0
EPS = 1e-6

LANES = 128
SUBLANES = 8
VMEM_LIMIT = 52 * 1024 * 1024

F32 = jnp.float32
BF16 = jnp.bfloat16
NEG = -1e30


def _rms_norm_rows(x, g):
    ms = jnp.mean(x * x, axis=-1, keepdims=True)
    return x * lax.rsqrt(ms + EPS) * g


FFN_TM = 512
FFN_TF = 512


def _ffn_kernel(x_ref, g_ref, wg_ref, wu_ref, wd_ref, o_ref, h_ref):
    j = pl.program_id(1)

    @pl.when(j == 0)
    def _():
        x = x_ref[...]
        h_ref[...] = _rms_norm_rows(x, g_ref[...]).astype(BF16)
        o_ref[...] = x

    h = h_ref[...]
    gate = jnp.dot(h, wg_ref[...], preferred_element_type=F32)
    up = jnp.dot(h, wu_ref[...], preferred_element_type=F32)
    a = (gate * jax.nn.sigmoid(gate) * (0.5 * up)).astype(BF16)
    o_ref[...] += jnp.dot(a, wd_ref[...], preferred_element_type=F32)


def _ffn(x, g, wg, wu, wd):
    n = x.shape[0]
    return pl.pallas_call(
        _ffn_kernel,
        grid=(n // FFN_TM, D_FF // FFN_TF),
        in_specs=[
            pl.BlockSpec((FFN_TM, D_MODEL), lambda i, j: (i, 0)),
            pl.BlockSpec((1, D_MODEL), lambda i, j: (0, 0)),
            pl.BlockSpec((D_MODEL, FFN_TF), lambda i, j: (0, j)),
            pl.BlockSpec((D_MODEL, FFN_TF), lambda i, j: (0, j)),
            pl.BlockSpec((FFN_TF, D_MODEL), lambda i, j: (j, 0)),
        ],
        out_specs=pl.BlockSpec((FFN_TM, D_MODEL), lambda i, j: (i, 0)),
        out_shape=jax.ShapeDtypeStruct((n, D_MODEL), F32),
        scratch_shapes=[pltpu.VMEM((FFN_TM, D_MODEL), BF16)],
        compiler_params=pltpu.CompilerParams(
            dimension_semantics=("parallel", "arbitrary"),
            vmem_limit_bytes=VMEM_LIMIT),
        name="ffn",
    )(x, g, wg, wu, wd)


INP_TM = 512
INP_TN = 1024
GLU_HALF = INP_TN // 2
HEAD_GROUP = 256


def _head_norm(z, gain, e):
    ss = z * z
    hi = ss.astype(BF16)
    lo = (ss - hi.astype(F32)).astype(BF16)
    outs = []
    for c in range(ATT_WIDTH // HEAD_GROUP):
        sl = slice(c * HEAD_GROUP, (c + 1) * HEAD_GROUP)
        tot = (jnp.dot(hi[:, sl], e, preferred_element_type=F32)
               + jnp.dot(lo[:, sl], e, preferred_element_type=F32))
        outs.append(z[:, sl] * lax.rsqrt(tot * (1.0 / HEAD_DIM) + EPS))
    return jnp.concatenate(outs, axis=1) * gain


def _inproj_kernel(x_ref, g_ref, w_ref, qg_ref, kg_ref, e_ref,
                   glu_ref, q_ref, k_ref, v_ref, h_ref):
    j = pl.program_id(1)

    @pl.when(j == 0)
    def _():
        h_ref[...] = _rms_norm_rows(x_ref[...], g_ref[...]).astype(BF16)

    z = jnp.dot(h_ref[...], w_ref[...], preferred_element_type=F32)

    @pl.when(j < 2)
    def _():
        glu_ref[...] = z[:, :GLU_HALF] * jax.nn.sigmoid(z[:, GLU_HALF:])

    @pl.when(j == 2)
    def _():
        q_ref[...] = _head_norm(z, qg_ref[...], e_ref[...]) * (HEAD_DIM ** -0.5)

    @pl.when(j == 3)
    def _():
        k_ref[...] = _head_norm(z, kg_ref[...], e_ref[...])

    @pl.when(j == 4)
    def _():
        v_ref[...] = z


def _inproj(x, g, w_perm, qg, kg, e):
    n = x.shape[0]
    att = jax.ShapeDtypeStruct((n, ATT_WIDTH), F32)
    att_spec = pl.BlockSpec((INP_TM, ATT_WIDTH), lambda i, j: (i, 0))
    const = lambda shape: pl.BlockSpec(shape, lambda i, j: (0, 0))
    return pl.pallas_call(
        _inproj_kernel,
        grid=(n // INP_TM, 5),
        in_specs=[
            pl.BlockSpec((INP_TM, D_MODEL), lambda i, j: (i, 0)),
            const((1, D_MODEL)),
            pl.BlockSpec((D_MODEL, INP_TN), lambda i, j: (0, j)),
            const((1, ATT_WIDTH)),
            const((1, ATT_WIDTH)),
            const((HEAD_GROUP, HEAD_GROUP)),
        ],
        out_specs=[
            pl.BlockSpec((INP_TM, GLU_HALF), lambda i, j: (i, jnp.minimum(j, 1))),
            att_spec, att_spec, att_spec,
        ],
        out_shape=[jax.ShapeDtypeStruct((n, CONV_CH), F32), att, att, att],
        scratch_shapes=[pltpu.VMEM((INP_TM, D_MODEL), BF16)],
        compiler_params=pltpu.CompilerParams(
            dimension_semantics=("parallel", "arbitrary"),
            vmem_limit_bytes=VMEM_LIMIT),
        name="in_proj",
    )(x, g, w_perm, qg, kg, e)


CONV_TS = 256
CONV_HALO = 32
CONV_RC = 32
CONV_PAD = CONV_HALO - (CONV_WIDTH - 1)
CONV_SH_ROWS = CONV_TS + CONV_HALO - SUBLANES


def _conv_kernel(cur_ref, prev_ref, w_ref, b_ref, g_ref, beta_ref, o_ref, buf_ref, sh_ref):
    i = pl.program_id(1)
    buf_ref[0:CONV_HALO, :] = jnp.where(i == 0, 0.0, prev_ref[0])
    buf_ref[CONV_HALO:, :] = cur_ref[0]
    for s in range(1, SUBLANES):
        sh_ref[s - 1] = buf_ref[s:s + CONV_SH_ROWS, :]

    bias = b_ref[...]
    gain = g_ref[...]
    beta = beta_ref[...]

    def chunk(c, carry):
        r0 = pl.multiple_of(c * CONV_RC, CONV_RC)
        acc = jnp.broadcast_to(bias, (CONV_RC, CONV_CH))
        for t in range(CONV_WIDTH):
            off = CONV_PAD + t
            phase, base = off % SUBLANES, off - off % SUBLANES
            if phase == 0:
                rows = buf_ref[pl.ds(r0 + base, CONV_RC), :]
            else:
                rows = sh_ref[phase - 1, pl.ds(r0 + base, CONV_RC), :]
            acc = acc + rows * w_ref[t:t + 1, :]
        mu = jnp.mean(acc, axis=-1, keepdims=True)
        cen = acc - mu
        var = jnp.mean(cen * cen, axis=-1, keepdims=True)
        y = cen * lax.rsqrt(var + EPS) * gain + beta
        o_ref[0, pl.ds(r0, CONV_RC), :] = (y * jax.nn.sigmoid(y)).astype(o_ref.dtype)
        return carry

    lax.fori_loop(0, CONV_TS // CONV_RC, chunk, 0)


def _conv(v, w, b, g, beta):
    bsz, s, _ = v.shape
    halo_blocks = CONV_TS // CONV_HALO
    const = lambda shape: pl.BlockSpec(shape, lambda bb, i: (0, 0))
    return pl.pallas_call(
        _conv_kernel,
        grid=(bsz, s // CONV_TS),
        in_specs=[
            pl.BlockSpec((1, CONV_TS, CONV_CH), lambda bb, i: (bb, i, 0)),
            pl.BlockSpec((1, CONV_HALO, CONV_CH),
                         lambda bb, i: (bb, jnp.maximum(i * halo_blocks - 1, 0), 0)),
            const((CONV_WIDTH, CONV_CH)),
            const((1, CONV_CH)), const((1, CONV_CH)), const((1, CONV_CH)),
        ],
        out_specs=pl.BlockSpec((1, CONV_TS, CONV_CH), lambda bb, i: (bb, i, 0)),
        out_shape=jax.ShapeDtypeStruct((bsz, s, CONV_CH), BF16),
        scratch_shapes=[
            pltpu.VMEM((CONV_TS + CONV_HALO, CONV_CH), F32),
            pltpu.VMEM((SUBLANES - 1, CONV_SH_ROWS, CONV_CH), F32),
        ],
        compiler_params=pltpu.CompilerParams(
            dimension_semantics=("parallel", "arbitrary"),
            vmem_limit_bytes=VMEM_LIMIT),
        name="conv",
    )(v, v, w, b, g, beta)


ATT_CH = 2048
ATT_QB = WIN
ATT_KB = 2 * WIN
ATT_TILES = ATT_CH // ATT_QB
ATT_UNROLL = 8


def _attn_kernel(slopes_ref, q_ref, kc_ref, vc_ref, o_ref,
                 qd, kd, vd, biasbuf, obuf, mbuf, rbuf):
    hp = pl.program_id(1)
    c = pl.program_id(2)
    first_chunk = c == 0

    lane =lax.broadcasted_iota(jnp.int32, (1, LANES), 1)
    in_a = lane < HEAD_DIM
    qi = lax.broadcasted_iota(jnp.int32, (ATT_QB, ATT_KB), 0)
    kj = lax.broadcasted_iota(jnp.int32, (ATT_QB, ATT_KB), 1)
    dist = WIN + qi - kj
    valid = (dist >= 0) & (dist <= WIN)
    valid_first = valid & (kj >= WIN)
    distf = dist.astype(F32)

    for bi, d in enumerate(DILATIONS):
        span = d * WIN
        nwin = ATT_CH // span + 1

        for hh in range(2):
            alibi = (-slopes_ref[2 * hp + hh] * d) * distf
            rows = slice(hh * ATT_QB, (hh + 1) * ATT_QB)
            biasbuf[bi, 0, rows, :] = jnp.where(valid, alibi, NEG)
            biasbuf[bi, 1, rows, :] = jnp.where(valid_first, alibi, NEG)

        def put_kv(k_rows, v_rows, dst, bi=bi):
            kd[bi, pl.ds(dst, ATT_QB), :] = k_rows.astype(BF16)
            vd[bi, 0, pl.ds(dst, ATT_QB), :] = jnp.where(in_a, v_rows, 1.0).astype(BF16)
            vd[bi, 1, pl.ds(dst, ATT_QB), :] = jnp.where(in_a, 1.0, v_rows).astype(BF16)

        def regroup_cur(t, carry, d=d, bi=bi, span=span, nwin=nwin, put_kv=put_kv):
            blk = t // d
            res = t - blk * d
            src = pl.ds(blk * span + res, ATT_QB, stride=d)
            qv = q_ref[0, src, :]
            q0 = pl.multiple_of(t * (2 * ATT_QB), 2 * ATT_QB)
            qd[bi, pl.ds(q0, ATT_QB), :] = jnp.where(in_a, qv, 0.0).astype(BF16)
            qd[bi, pl.ds(q0 + ATT_QB, ATT_QB), :] = jnp.where(in_a, 0.0, qv).astype(BF16)
            put_kv(kc_ref[0, src, :], vc_ref[0, src, :],
                   pl.multiple_of((res * nwin + blk + 1) * ATT_QB, ATT_QB))
            return carry

        def carry_prev(res, carry, bi=bi, nwin=nwin):
            dst = pl.ds(pl.multiple_of(res * nwin * ATT_QB, ATT_QB), ATT_QB)
            src = pl.ds(pl.multiple_of((res * nwin + nwin - 1) * ATT_QB, ATT_QB), ATT_QB)
            kd[bi, dst, :] = kd[bi, src, :]
            vd[bi, 0, dst, :] = vd[bi, 0, src, :]
            vd[bi, 1, dst, :] = vd[bi, 1, src, :]
            return carry

        def zero_prev(res, carry, bi=bi, nwin=nwin):
            dst = pl.ds(pl.multiple_of(res * nwin * ATT_QB, ATT_QB), ATT_QB)
            zeros = jnp.zeros((ATT_QB, LANES), BF16)
            kd[bi, dst, :] = zeros
            vd[bi, 0, dst, :] = zeros
            vd[bi, 1, dst, :] = zeros
            return carry

        @pl.when(first_chunk)
        def _(d=d, zero_prev=zero_prev):
            lax.fori_loop(0, d, zero_prev, 0)

        @pl.when(jnp.logical_not(first_chunk))
        def _(d=d, carry_prev=carry_prev):
            lax.fori_loop(0, d, carry_prev, 0)

        lax.fori_loop(0, ATT_TILES, regroup_cur, 0, unroll=2)

        def tile(t, carry, d=d, bi=bi, span=span, nwin=nwin):
            blk = t // d
            res = t - blk * d
            q0 = pl.multiple_of(t * (2 * ATT_QB), 2 * ATT_QB)
            k0 = pl.multiple_of((res * nwin + blk) * ATT_QB, ATT_QB)
            first = jnp.logical_and(first_chunk, blk == 0).astype(jnp.int32)
            s = lax.dot_general(qd[bi, pl.ds(q0, 2 * ATT_QB), :], kd[bi, pl.ds(k0, ATT_KB), :],
                                (((1,), (1,)), ((), ())), preferred_element_type=F32)
            s = s + biasbuf[bi, first]
            m = jnp.max(s, axis=-1, keepdims=True)
            p = jnp.exp(s - m).astype(BF16)
            dst = pl.ds(blk * span + res, ATT_QB, stride=d)
            for hh in range(2):
                rows = slice(hh * ATT_QB, (hh + 1) * ATT_QB)
                oe = jnp.dot(p[rows], vd[bi, hh, pl.ds(k0, ATT_KB), :], preferred_element_type=F32)
                obuf[bi, hh, dst, :] = oe
                mbuf[bi, hh, dst, :] = jnp.broadcast_to(m[rows], (ATT_QB, LANES))
            return carry

        lax.fori_loop(0, ATT_TILES, tile, 0, unroll=ATT_UNROLL)

    for hh in range(2):
        def combine(t, carry, hh=hh):
            rows = pl.ds(pl.multiple_of(t * ATT_QB, ATT_QB), ATT_QB)
            m0, m1, m2 = mbuf[0, hh, rows, :], mbuf[1, hh, rows, :], mbuf[2, hh, rows, :]
            mm = jnp.maximum(jnp.maximum(m0, m1), m2)
            tot = (jnp.exp(m0 - mm) * obuf[0, hh, rows, :] + jnp.exp(m1 - mm) * obuf[1, hh, rows, :]
                   + jnp.exp(m2 - mm) * obuf[2, hh, rows, :])
            res = tot / pltpu.roll(tot, HEAD_DIM, axis=1)
            if hh == 0:
                rbuf[rows, :] = res
            else:
                o_ref[0, rows, :] = jnp.where(in_a, rbuf[rows, :], res).astype(o_ref.dtype)
            return carry

        lax.fori_loop(0, ATT_TILES, combine, 0, unroll=2)


def _attn(q, k, v, slopes):
    bsz, s, _ = q.shape
    blk = (1, ATT_CH, LANES)
    nb = len(DILATIONS)
    cur = pl.BlockSpec(blk, lambda b, h, c, sl: (b, c, h))
    grid_spec = pltpu.PrefetchScalarGridSpec(
        num_scalar_prefetch=1,
        grid=(bsz, ATT_WIDTH // LANES, s // ATT_CH),
        in_specs=[cur, cur, cur],
        out_specs=cur,
        scratch_shapes=[
            pltpu.VMEM((nb, 2 * ATT_CH, LANES), BF16),
            pltpu.VMEM((nb, 2 * ATT_CH, LANES), BF16),
            pltpu.VMEM((nb, 2, 2 * ATT_CH, LANES), BF16),
            pltpu.VMEM((nb, 2, 2 * ATT_QB, ATT_KB), F32),
            pltpu.VMEM((nb, 2, ATT_CH, LANES), F32),
            pltpu.VMEM((nb, 2, ATT_CH, LANES), F32),
            pltpu.VMEM((ATT_CH, LANES), F32),
        ],
    )
    return pl.pallas_call(
        _attn_kernel,
        grid_spec=grid_spec,
        out_shape=jax.ShapeDtypeStruct((bsz, s, ATT_WIDTH), BF16),
        compiler_params=pltpu.CompilerParams(
            dimension_semantics=("parallel", "arbitrary", "arbitrary"),
            vmem_limit_bytes=VMEM_LIMIT),
        name="attn",
    )(slopes, q, k, v)


OUT_TM = 512


def _outproj_kernel(x_ref, yc_ref, ya_ref, wc_ref, wa_ref, o_ref):
    o_ref[...] = (x_ref[...]
                  + jnp.dot(yc_ref[...], wc_ref[...], preferred_element_type=F32)
                  + jnp.dot(ya_ref[...], wa_ref[...], preferred_element_type=F32))


def _outproj(x, yc, ya, wc, wa):
    n = x.shape[0]
    return pl.pallas_call(
        _outproj_kernel,
        grid=(n // OUT_TM,),
        in_specs=[
            pl.BlockSpec((OUT_TM, D_MODEL), lambda i: (i, 0)),
            pl.BlockSpec((OUT_TM, CONV_CH), lambda i: (i, 0)),
            pl.BlockSpec((OUT_TM, ATT_WIDTH), lambda i: (i, 0)),
            pl.BlockSpec((CONV_CH, D_MODEL), lambda i: (0, 0)),
            pl.BlockSpec((ATT_WIDTH, D_MODEL), lambda i: (0, 0)),
        ],
        out_specs=pl.BlockSpec((OUT_TM, D_MODEL), lambda i: (i, 0)),
        out_shape=jax.ShapeDtypeStruct((n, D_MODEL), F32),
        compiler_params=pltpu.CompilerParams(
            dimension_semantics=("parallel",),
            vmem_limit_bytes=VMEM_LIMIT),
        name="out_proj",
    )(x, yc, ya, wc, wa)


def _layer(x2d, bsz, s, p):
    n = bsz * s
    x2d = _ffn(x2d, p["ffn1_norm_g"], p["ffn1_w_gate"], p["ffn1_w_up"], p["ffn1_w_down"])
    glu, q, k, v = _inproj(x2d, p["mix_norm_g"], p["w_in"], p["q_norm_g"], p["k_norm_g"], p["head_ones"])
    yc = _conv(glu.reshape(bsz, s, CONV_CH), p["conv_w_dw"], p["conv_b_dw"], p["conv_ln_g"], p["conv_ln_b"])
    ya = _attn(q.reshape(bsz, s, ATT_WIDTH), k.reshape(bsz, s, ATT_WIDTH), v.reshape(bsz, s, ATT_WIDTH),
               p["slopes"])
    x2d = _outproj(x2d, yc.reshape(n, CONV_CH), ya.reshape(n, ATT_WIDTH), p["w_out_conv"], p["w_out_att"])
    return _ffn(x2d, p["ffn2_norm_g"], p["ffn2_w_gate"], p["ffn2_w_up"], p["ffn2_w_down"])


def kernel(x, ffn1_norm_g, ffn1_w_gate, ffn1_w_up, ffn1_w_down, mix_norm_g, w_in, conv_w_dw, conv_b_dw,
           conv_ln_g, conv_ln_b, q_norm_g, k_norm_g, w_out, ffn2_norm_g, ffn2_w_gate, ffn2_w_up, ffn2_w_down):
    bsz, s, _ = x.shape
    depth = w_in.shape[0]
    row = lambda a: a.reshape(1, -1).astype(F32)
    head_ids = jnp.arange(HEAD_GROUP) // HEAD_DIM
    head_ones = (head_ids[:, None] == head_ids[None, :]).astype(BF16)
    slopes = 2.0 ** (-ALIBI_MAX_BIAS * jnp.arange(1, N_HEADS + 1, dtype=F32) / N_HEADS)
    x2d = x.reshape(bsz * s, D_MODEL)
    for l in range(depth):
        w = w_in[l]
        wa, wg = w[:, :CONV_CH], w[:, CONV_CH:2 * CONV_CH]
        w_perm = jnp.concatenate(
            [wa[:, :GLU_HALF], wg[:, :GLU_HALF], wa[:, GLU_HALF:], wg[:, GLU_HALF:], w[:, 2 * CONV_CH:]],
            axis=1).astype(BF16)
        p = dict(
            ffn1_norm_g=row(ffn1_norm_g[l]), ffn1_w_gate=ffn1_w_gate[l].astype(BF16),
            ffn1_w_up=ffn1_w_up[l].astype(BF16), ffn1_w_down=ffn1_w_down[l].astype(BF16),
            mix_norm_g=row(mix_norm_g[l]), w_in=w_perm,
            q_norm_g=row(jnp.tile(q_norm_g[l], N_HEADS)), k_norm_g=row(jnp.tile(k_norm_g[l], N_HEADS)),
            head_ones=head_ones, slopes=slopes,
            conv_w_dw=conv_w_dw[l].astype(F32), conv_b_dw=row(conv_b_dw[l]),
            conv_ln_g=row(conv_ln_g[l]), conv_ln_b=row(conv_ln_b[l]),
            w_out_conv=w_out[l, :CONV_CH].astype(BF16), w_out_att=w_out[l, CONV_CH:].astype(BF16),
            ffn2_norm_g=row(ffn2_norm_g[l]), ffn2_w_gate=ffn2_w_gate[l].astype(BF16),
            ffn2_w_up=ffn2_w_up[l].astype(BF16), ffn2_w_down=ffn2_w_down[l].astype(BF16),
        )
        x2d = _layer(x2d, bsz, s, p)
    return x2d.reshape(bsz, s, D_MODEL)
```

```python
import functools

import jax
import jax.numpy as jnp
from jax import lax
from jax.experimental import pallas as pl
from jax.experimental.pallas import tpu as pltpu

D_MODEL = 2048
D_FF = 5632
CONV_CH = 1024
CONV_WIDTH = 31
N_HEADS = 16
HEAD_DIM = 64
ATT_WIDTH = N_HEADS * HEAD_DIM
DILATIONS = (1, 4, 16)
WIN = 128
ALIBI_MAX_BIAS = 8.0
EPS = 1e-6

LANES = 128
SUBLANES = 8
VMEM_LIMIT = 56 * 1024 * 1024

F32 = jnp.float32
BF16 = jnp.bfloat16
NEG = -1e30


def _rms_norm_rows(x, g):
    ms = jnp.mean(x * x, axis=-1, keepdims=True)
    return x * lax.rsqrt(ms + EPS) * g


FFN_TM = 1024
FFN_TF = 512


def _ffn_kernel(x_ref, g_ref, wg_ref, wu_ref, wd_ref, o_ref, h_ref):
    j = pl.program_id(1)

    @pl.when(j == 0)
    def _():
        x = x_ref[...]
        h_ref[...] = _rms_norm_rows(x, g_ref[...]).astype(BF16)
        o_ref[...] = x

    h = h_ref[...]
    gate = jnp.dot(h, wg_ref[...], preferred_element_type=F32)
    up = jnp.dot(h, wu_ref[...], preferred_element_type=F32)
    a = (gate * jax.nn.sigmoid(gate) * (0.5 * up)).astype(BF16)
    o_ref[...] += jnp.dot(a, wd_ref[...], preferred_element_type=F32)


def _ffn(x, g, wg, wu, wd):
    n = x.shape[0]
    return pl.pallas_call(
        _ffn_kernel,
        grid=(n // FFN_TM, D_FF // FFN_TF),
        in_specs=[
            pl.BlockSpec((FFN_TM, D_MODEL), lambda i, j: (i, 0)),
            pl.BlockSpec((1, D_MODEL), lambda i, j: (0, 0)),
            pl.BlockSpec((D_MODEL, FFN_TF), lambda i, j: (0, j)),
            pl.BlockSpec((D_MODEL, FFN_TF), lambda i, j: (0, j)),
            pl.BlockSpec((FFN_TF, D_MODEL), lambda i, j: (j, 0)),
        ],
        out_specs=pl.BlockSpec((FFN_TM, D_MODEL), lambda i, j: (i, 0)),
        out_shape=jax.ShapeDtypeStruct((n, D_MODEL), F32),
        scratch_shapes=[pltpu.VMEM((FFN_TM, D_MODEL), BF16)],
        compiler_params=pltpu.CompilerParams(
            dimension_semantics=("parallel", "arbitrary"),
            vmem_limit_bytes=VMEM_LIMIT),
        name="ffn",
    )(x, g, wg, wu, wd)


INP_TM = 256
HEAD_GROUP = 256


def _head_norm(z, gain, e):
    ss = z * z
    hi = ss.astype(BF16)
    lo = (ss - hi.astype(F32)).astype(BF16)
    outs = []
    for c in range(ATT_WIDTH // HEAD_GROUP):
        sl = slice(c * HEAD_GROUP, (c + 1) * HEAD_GROUP)
        tot = (jnp.dot(hi[:, sl], e, preferred_element_type=F32)
               + jnp.dot(lo[:, sl], e, preferred_element_type=F32))
        outs.append(z[:, sl] * lax.rsqrt(tot * (1.0 / HEAD_DIM) + EPS))
    return jnp.concatenate(outs, axis=1) * gain


def _inproj_kernel(x_ref, g_ref, w_ref, qg_ref, kg_ref, e_ref, glu_ref, q_ref, k_ref, v_ref):
    x = x_ref[...]
    r = lax.rsqrt(jnp.mean(x * x, axis=-1, keepdims=True) + EPS)
    xg = (x * g_ref[...]).astype(BF16)

    def proj(col0):
        return jnp.dot(xg, w_ref[:, col0:col0 + CONV_CH], preferred_element_type=F32) * r

    glu_ref[...] = proj(0) * jax.nn.sigmoid(proj(CONV_CH))
    q_ref[...] = _head_norm(proj(2 * CONV_CH), qg_ref[...], e_ref[...]) * (HEAD_DIM ** -0.5)
    k_ref[...] = _head_norm(proj(2 * CONV_CH + ATT_WIDTH), kg_ref[...], e_ref[...])
    v_ref[...] = proj(2 * CONV_CH + 2 * ATT_WIDTH)


def _inproj(x, g, w, qg, kg, e):
    n = x.shape[0]
    d_in = w.shape[1]
    out = jax.ShapeDtypeStruct((n, ATT_WIDTH), F32)
    out_spec = pl.BlockSpec((INP_TM, ATT_WIDTH), lambda i: (i, 0))
    const = lambda shape, **kw: pl.BlockSpec(shape, lambda i: (0, 0), **kw)
    return pl.pallas_call(
        _inproj_kernel,
        grid=(n // INP_TM,),
        in_specs=[
            pl.BlockSpec((INP_TM, D_MODEL), lambda i: (i, 0)),
            const((1, D_MODEL)),
            const((D_MODEL, d_in), pipeline_mode=pl.Buffered(1)),
            const((1, ATT_WIDTH)),
            const((1, ATT_WIDTH)),
            const((HEAD_GROUP, HEAD_GROUP)),
        ],
        out_specs=[out_spec, out_spec, out_spec, out_spec],
        out_shape=[out, out, out, out],
        compiler_params=pltpu.CompilerParams(
            dimension_semantics=("parallel",),
            vmem_limit_bytes=VMEM_LIMIT),
        name="in_proj",
    )(x, g, w, qg, kg, e)


CONV_TS = 256
CONV_HALO = 32
CONV_RC = 32
CONV_PAD = CONV_HALO - (CONV_WIDTH - 1)
CONV_SH_ROWS = CONV_TS + CONV_HALO - SUBLANES


def _conv_kernel(cur_ref, prev_ref, w_ref, b_ref, g_ref, beta_ref, o_ref, buf_ref, sh_ref):
    i = pl.program_id(1)
    buf_ref[0:CONV_HALO, :] = jnp.where(i == 0, 0.0, prev_ref[0])
    buf_ref[CONV_HALO:, :] = cur_ref[0]
    for s in range(1, SUBLANES):
        sh_ref[s - 1] = buf_ref[s:s + CONV_SH_ROWS, :]

    bias = b_ref[...]
    gain = g_ref[...]
    beta = beta_ref[...]

    def chunk(c, carry):
        r0 = pl.multiple_of(c * CONV_RC, CONV_RC)
        acc = jnp.broadcast_to(bias, (CONV_RC, CONV_CH))
        for t in range(CONV_WIDTH):
            off = CONV_PAD + t
            phase, base = off % SUBLANES, off - off % SUBLANES
            if phase == 0:
                rows = buf_ref[pl.ds(r0 + base, CONV_RC), :]
            else:
                rows = sh_ref[phase - 1, pl.ds(r0 + base, CONV_RC), :]
            acc = acc + rows * w_ref[t:t + 1, :]
        mu = jnp.mean(acc, axis=-1, keepdims=True)
        cen = acc - mu
        var = jnp.mean(cen * cen, axis=-1, keepdims=True)
        y = cen * lax.rsqrt(var + EPS) * gain + beta
        o_ref[0, pl.ds(r0, CONV_RC), :] = (y * jax.nn.sigmoid(y)).astype(o_ref.dtype)
        return carry

    lax.fori_loop(0, CONV_TS // CONV_RC, chunk, 0)


def _conv(v, w, b, g, beta):
    bsz, s, _ = v.shape
    halo_blocks = CONV_TS // CONV_HALO
    const = lambda shape: pl.BlockSpec(shape, lambda bb, i: (0, 0))
    return pl.pallas_call(
        _conv_kernel,
        grid=(bsz, s // CONV_TS),
        in_specs=[
            pl.BlockSpec((1, CONV_TS, CONV_CH), lambda bb, i: (bb, i, 0)),
            pl.BlockSpec((1, CONV_HALO, CONV_CH),
                         lambda bb, i: (bb, jnp.maximum(i * halo_blocks - 1, 0), 0)),
            const((CONV_WIDTH, CONV_CH)),
            const((1, CONV_CH)), const((1, CONV_CH)), const((1, CONV_CH)),
        ],
        out_specs=pl.BlockSpec((1, CONV_TS, CONV_CH), lambda bb, i: (bb, i, 0)),
        out_shape=jax.ShapeDtypeStruct((bsz, s, CONV_CH), BF16),
        scratch_shapes=[
            pltpu.VMEM((CONV_TS + CONV_HALO, CONV_CH), F32),
            pltpu.VMEM((SUBLANES - 1, CONV_SH_ROWS, CONV_CH), F32),
        ],
        compiler_params=pltpu.CompilerParams(
            dimension_semantics=("parallel", "arbitrary"),
            vmem_limit_bytes=VMEM_LIMIT),
        name="conv",
    )(v, v, w, b, g, beta)


ATT_CH = 2048
ATT_QB = WIN
ATT_KB = 2 * WIN
ATT_TILES = ATT_CH // ATT_QB
ATT_STAGE = 4
ATT_QUART = ATT_CH // ATT_STAGE
ATT_UNROLL = 16


def _attn_kernel(slopes_ref, q_ref, kc_ref, vc_ref, o_ref,
                 tmp, qd, kd, vd, biasbuf, xbuf, ybuf, mbuf):
    hp = pl.program_id(1)
    c = pl.program_id(2)
    first_chunk = c == 0

    lane = lax.broadcasted_iota(jnp.int32, (1, LANES), 1)
    in_a = lane < HEAD_DIM
    qi = lax.broadcasted_iota(jnp.int32, (ATT_QB, ATT_KB), 0)
    kj = lax.broadcasted_iota(jnp.int32, (ATT_QB, ATT_KB), 1)
    dist = WIN + qi - kj
    valid = (dist >= 0) & (dist <= WIN)
    valid_first = valid & (kj >= WIN)
    distf = dist.astype(F32)

    for ai, ref in enumerate((q_ref, kc_ref, vc_ref)):
        for a in range(ATT_STAGE):
            tmp[ai, a * ATT_QUART:(a + 1) * ATT_QUART, :] = ref[0, pl.ds(a, ATT_QUART, stride=ATT_STAGE), :]

    for bi, d in enumerate(DILATIONS):
        span = d * WIN
        nwin = ATT_CH // span + 1

        for hh in range(2):
            alibi = (-slopes_ref[2 * hp + hh] * d) * distf
            rows = slice(hh * ATT_QB, (hh + 1) * ATT_QB)
            biasbuf[bi, 0, rows, :] = jnp.where(valid, alibi, NEG)
            biasbuf[bi, 1, rows, :] = jnp.where(valid_first, alibi, NEG)

        def load_qkv(blk, res, d=d):
            if d == 1:
                idx = pl.ds(pl.multiple_of(blk * ATT_QB, ATT_QB), ATT_QB)
                return q_ref[0, idx, :], kc_ref[0, idx, :], vc_ref[0, idx, :]
            if d == ATT_STAGE:
                idx = pl.ds(pl.multiple_of(res * ATT_QUART + blk * ATT_QB, ATT_QB), ATT_QB)
            else:
                hi = res // ATT_STAGE
                idx = pl.ds((res - hi * ATT_STAGE) * ATT_QUART + hi, ATT_QB, stride=ATT_STAGE)
            return tmp[0, idx, :], tmp[1, idx, :], tmp[2, idx, :]

        def regroup_cur(t, carry, d=d, bi=bi, nwin=nwin, load_qkv=load_qkv):
            blk = t // d
            res = t - blk * d
            qv, kv, vv = load_qkv(blk, res)
            q0 = pl.multiple_of(t * (2 * ATT_QB), 2 * ATT_QB)
            qd[bi, pl.ds(q0, ATT_QB), :] = jnp.where(in_a, qv, 0.0).astype(BF16)
            qd[bi, pl.ds(q0 + ATT_QB, ATT_QB), :] = jnp.where(in_a, 0.0, qv).astype(BF16)
            dst = pl.ds(pl.multiple_of((res * nwin + blk + 1) * ATT_QB, ATT_QB), ATT_QB)
            kd[bi, dst, :] = kv.astype(BF16)
            vd[bi, 0, dst, :] = jnp.where(in_a, vv, 1.0).astype(BF16)
            vd[bi, 1, dst, :] = jnp.where(in_a, 1.0, vv).astype(BF16)
            return carry

        def carry_prev(res, carry, bi=bi, nwin=nwin):
            dst = pl.ds(pl.multiple_of(res * nwin * ATT_QB, ATT_QB), ATT_QB)
            src = pl.ds(pl.multiple_of((res * nwin + nwin - 1) * ATT_QB, ATT_QB), ATT_QB)
            kd[bi, dst, :] = kd[bi, src, :]
            vd[bi, 0, dst, :] = vd[bi, 0, src, :]
            vd[bi, 1, dst, :] = vd[bi, 1, src, :]
            return carry

        def zero_prev(res, carry, bi=bi, nwin=nwin):
            dst = pl.ds(pl.multiple_of(res * nwin * ATT_QB, ATT_QB), ATT_QB)
            zeros = jnp.zeros((ATT_QB, LANES), BF16)
            kd[bi, dst, :] = zeros
            vd[bi, 0, dst, :] = zeros
            vd[bi, 1, dst, :] = zeros
            return carry

        @pl.when(first_chunk)
        def _(d=d, zero_prev=zero_prev):
            lax.fori_loop(0, d, zero_prev, 0)

        @pl.when(jnp.logical_not(first_chunk))
        def _(d=d, carry_prev=carry_prev):
            lax.fori_loop(0, d, carry_prev, 0)

        lax.fori_loop(0, ATT_TILES, regroup_cur, 0, unroll=2)

        def tile(t, carry, d=d, bi=bi, span=span, nwin=nwin):
            blk = t // d
            res = t - blk * d
            q0 = pl.multiple_of(t * (2 * ATT_QB), 2 * ATT_QB)
            k0 = pl.multiple_of((res * nwin + blk) * ATT_QB, ATT_QB)
            first = jnp.logical_and(first_chunk, blk == 0).astype(jnp.int32)
            s = lax.dot_general(qd[bi, pl.ds(q0, 2 * ATT_QB), :], kd[bi, pl.ds(k0, ATT_KB), :],
                                (((1,), (1,)), ((), ())), preferred_element_type=F32)
            s = s + biasbuf[bi, first]
            m = jnp.max(s, axis=-1, keepdims=True)
            p = jnp.exp(s - m).astype(BF16)
            oa = jnp.dot(p[:ATT_QB], vd[bi, 0, pl.ds(k0, ATT_KB), :], preferred_element_type=F32)
            ob = jnp.dot(p[ATT_QB:], vd[bi, 1, pl.ds(k0, ATT_KB), :], preferred_element_type=F32)
            dst = pl.ds(blk * span + res, ATT_QB, stride=d)
            xbuf[bi, dst, :] = jnp.where(in_a, oa, ob)
            ybuf[bi, dst, :] = jnp.where(in_a, ob, oa)
            ma = jnp.broadcast_to(m[:ATT_QB], (ATT_QB, LANES))
            mb = jnp.broadcast_to(m[ATT_QB:], (ATT_QB, LANES))
            mbuf[bi, 0, dst, :] = jnp.where(in_a, ma, mb)
            mbuf[bi, 1, dst, :] = jnp.where(in_a, mb, ma)
            return carry

        lax.fori_loop(0, ATT_TILES, tile, 0, unroll=ATT_UNROLL)

    def combine(t, carry):
        rows = pl.ds(pl.multiple_of(t * ATT_QB, ATT_QB), ATT_QB)
        def weighted_sum(which, vals):
            ms = [mbuf[bi, which, rows, :] for bi in range(len(DILATIONS))]
            mm = functools.reduce(jnp.maximum, ms)
            return sum(jnp.exp(mb - mm) * vals[bi, rows, :] for bi, mb in enumerate(ms))

        num = weighted_sum(0, xbuf)
        den = weighted_sum(1, ybuf)
        o_ref[0, rows, :] = (num / pltpu.roll(den, HEAD_DIM, axis=1)).astype(o_ref.dtype)
        return carry

    lax.fori_loop(0, ATT_TILES, combine, 0, unroll=2)


def _attn(q, k, v, slopes):
    bsz, s, _ = q.shape
    blk = (1, ATT_CH, LANES)
    nb = len(DILATIONS)
    cur = pl.BlockSpec(blk, lambda b, h, c, sl: (b, c, h))
    grid_spec = pltpu.PrefetchScalarGridSpec(
        num_scalar_prefetch=1,
        grid=(bsz, ATT_WIDTH // LANES, s // ATT_CH),
        in_specs=[cur, cur, cur],
        out_specs=cur,
        scratch_shapes=[
            pltpu.VMEM((3, ATT_CH, LANES), F32),
            pltpu.VMEM((nb, 2 * ATT_CH, LANES), BF16),
            pltpu.VMEM((nb, 2 * ATT_CH, LANES), BF16),
            pltpu.VMEM((nb, 2, 2 * ATT_CH, LANES), BF16),
            pltpu.VMEM((nb, 2, 2 * ATT_QB, ATT_KB), F32),
            pltpu.VMEM((nb, ATT_CH, LANES), F32),
            pltpu.VMEM((nb, ATT_CH, LANES), F32),
            pltpu.VMEM((nb, 2, ATT_CH, LANES), F32),
        ],
    )
    return pl.pallas_call(
        _attn_kernel,
        grid_spec=grid_spec,
        out_shape=jax.ShapeDtypeStruct((bsz, s, ATT_WIDTH), BF16),
        compiler_params=pltpu.CompilerParams(
            dimension_semantics=("parallel", "arbitrary", "arbitrary"),
            vmem_limit_bytes=VMEM_LIMIT),
        name="attn",
    )(slopes, q, k, v)


OUT_TM = 512


def _outproj_kernel(x_ref, yc_ref, ya_ref, wc_ref, wa_ref, o_ref):
    o_ref[...] = (x_ref[...]
                  + jnp.dot(yc_ref[...], wc_ref[...], preferred_element_type=F32)
                  + jnp.dot(ya_ref[...], wa_ref[...], preferred_element_type=F32))


def _outproj(x, yc, ya, wc, wa):
    n = x.shape[0]
    return pl.pallas_call(
        _outproj_kernel,
        grid=(n // OUT_TM,),
        in_specs=[
            pl.BlockSpec((OUT_TM, D_MODEL), lambda i: (i, 0)),
            pl.BlockSpec((OUT_TM, CONV_CH), lambda i: (i, 0)),
            pl.BlockSpec((OUT_TM, ATT_WIDTH), lambda i: (i, 0)),
            pl.BlockSpec((CONV_CH, D_MODEL), lambda i: (0, 0)),
            pl.BlockSpec((ATT_WIDTH, D_MODEL), lambda i: (0, 0)),
        ],
        out_specs=pl.BlockSpec((OUT_TM, D_MODEL), lambda i: (i, 0)),
        out_shape=jax.ShapeDtypeStruct((n, D_MODEL), F32),
        compiler_params=pltpu.CompilerParams(
            dimension_semantics=("parallel",),
            vmem_limit_bytes=VMEM_LIMIT),
        name="out_proj",
    )(x, yc, ya, wc, wa)


def _layer(x2d, bsz, s, p):
    n = bsz * s
    x2d = _ffn(x2d, p["ffn1_norm_g"], p["ffn1_w_gate"], p["ffn1_w_up"], p["ffn1_w_down"])
    glu, q, k, v = _inproj(x2d, p["mix_norm_g"], p["w_in"], p["q_norm_g"], p["k_norm_g"], p["head_ones"])
    yc = _conv(glu.reshape(bsz, s, CONV_CH), p["conv_w_dw"], p["conv_b_dw"], p["conv_ln_g"], p["conv_ln_b"])
    ya = _attn(q.reshape(bsz, s, ATT_WIDTH), k.reshape(bsz, s, ATT_WIDTH), v.reshape(bsz, s, ATT_WIDTH),
               p["slopes"])
    x2d = _outproj(x2d, yc.reshape(n, CONV_CH), ya.reshape(n, ATT_WIDTH), p["w_out_conv"], p["w_out_att"])
    return _ffn(x2d, p["ffn2_norm_g"], p["ffn2_w_gate"], p["ffn2_w_up"], p["ffn2_w_down"])


def kernel(x, ffn1_norm_g, ffn1_w_gate, ffn1_w_up, ffn1_w_down, mix_norm_g, w_in, conv_w_dw, conv_b_dw,
           conv_ln_g, conv_ln_b, q_norm_g, k_norm_g, w_out, ffn2_norm_g, ffn2_w_gate, ffn2_w_up, ffn2_w_down):
    bsz, s, _ = x.shape
    depth = w_in.shape[0]
    row = lambda a: a.reshape(1, -1).astype(F32)
    head_ids = jnp.arange(HEAD_GROUP) // HEAD_DIM
    head_ones = (head_ids[:, None] == head_ids[None, :]).astype(BF16)
    slopes = 2.0 ** (-ALIBI_MAX_BIAS * jnp.arange(1, N_HEADS + 1, dtype=F32) / N_HEADS)
    x2d = x.reshape(bsz * s, D_MODEL)
    for l in range(depth):
        p = dict(
            ffn1_norm_g=row(ffn1_norm_g[l]), ffn1_w_gate=ffn1_w_gate[l].astype(BF16),
            ffn1_w_up=ffn1_w_up[l].astype(BF16), ffn1_w_down=ffn1_w_down[l].astype(BF16),
            mix_norm_g=row(mix_norm_g[l]), w_in=w_in[l].astype(BF16),
            q_norm_g=row(jnp.tile(q_norm_g[l], N_HEADS)), k_norm_g=row(jnp.tile(k_norm_g[l], N_HEADS)),
            head_ones=head_ones, slopes=slopes,
            conv_w_dw=conv_w_dw[l].astype(F32), conv_b_dw=row(conv_b_dw[l]),
            conv_ln_g=row(conv_ln_g[l]), conv_ln_b=row(conv_ln_b[l]),
            w_out_conv=w_out[l, :CONV_CH].astype(BF16), w_out_att=w_out[l, CONV_CH:].astype(BF16),
            ffn2_norm_g=row(ffn2_norm_g[l]), ffn2_w_gate=ffn2_w_gate[l].astype(BF16),
            ffn2_w_up=ffn2_w_up[l].astype(BF16), ffn2_w_down=ffn2_w_down[l].astype(BF16),
        )
        x2d = _layer(x2d, bsz, s, p)
    return x2d.reshape(bsz, s, D_MODEL)
```

```python
import functools

import jax
import jax.numpy as jnp
from jax import lax
from jax.experimental import pallas as pl
from jax.experimental.pallas import tpu as pltpu

D_MODEL = 2048
D_FF = 5632
CONV_CH = 1024
CONV_WIDTH = 31
N_HEADS = 16
HEAD_DIM = 64
ATT_WIDTH = N_HEADS * HEAD_DIM
DILATIONS = (1, 4, 16)
WIN = 128
ALIBI_MAX_BIAS = 8.0
EPS = 1e-6

LANES = 128
SUBLANES = 8
VMEM_LIMIT = 56 * 1024 * 1024

F32 = jnp.float32
BF16 = jnp.bfloat16
NEG = -1e30


def _rms_norm_rows(x, g):
    ms = jnp.mean(x * x, axis=-1, keepdims=True)
    return x * lax.rsqrt(ms + EPS) * g


FFN_TM = 1024
FFN_TF = 512


def _ffn_kernel(x_ref, g_ref, wg_ref, wu_ref, wd_ref, o_ref, h_ref):
    j = pl.program_id(1)

    @pl.when(j == 0)
    def _():
        x = x_ref[...]
        h_ref[...] = _rms_norm_rows(x, g_ref[...]).astype(BF16)
        o_ref[...] = x

    h = h_ref[...]
    gate = jnp.dot(h, wg_ref[...], preferred_element_type=F32)
    up = jnp.dot(h, wu_ref[...], preferred_element_type=F32)
    a = (gate * jax.nn.sigmoid(gate) * (0.5 * up)).astype(BF16)
    o_ref[...] += jnp.dot(a, wd_ref[...], preferred_element_type=F32)


def _ffn(x, g, wg, wu, wd):
    n = x.shape[0]
    return pl.pallas_call(
        _ffn_kernel,
        grid=(n // FFN_TM, D_FF // FFN_TF),
        in_specs=[
            pl.BlockSpec((FFN_TM, D_MODEL), lambda i, j: (i, 0)),
            pl.BlockSpec((1, D_MODEL), lambda i, j: (0, 0)),
            pl.BlockSpec((D_MODEL, FFN_TF), lambda i, j: (0, j)),
            pl.BlockSpec((D_MODEL, FFN_TF), lambda i, j: (0, j)),
            pl.BlockSpec((FFN_TF, D_MODEL), lambda i, j: (j, 0)),
        ],
        out_specs=pl.BlockSpec((FFN_TM, D_MODEL), lambda i, j: (i, 0)),
        out_shape=jax.ShapeDtypeStruct((n, D_MODEL), F32),
        scratch_shapes=[pltpu.VMEM((FFN_TM, D_MODEL), BF16)],
        compiler_params=pltpu.CompilerParams(
            dimension_semantics=("parallel", "arbitrary"),
            vmem_limit_bytes=VMEM_LIMIT),
        name="ffn",
    )(x, g, wg, wu, wd)


INP_TM = 256
HEAD_GROUP = 256


def _head_norm_group(z, gain, e):
    ss = z * z
    hi = ss.astype(BF16)
    lo = (ss - hi.astype(F32)).astype(BF16)
    tot = jnp.dot(hi, e, preferred_element_type=F32) + jnp.dot(lo, e, preferred_element_type=F32)
    return z * lax.rsqrt(tot * (1.0 / HEAD_DIM) + EPS) * gain


CONV_TS = INP_TM
CONV_HALO = 32
CONV_RC = 32
CONV_PAD = CONV_HALO - (CONV_WIDTH - 1)
CONV_SH_ROWS = CONV_TS + CONV_HALO - SUBLANES


def _mixin_kernel(tiles_per_seq, x_ref, g_ref, w_ref, qg_ref, kg_ref, e_ref, cw_ref, cb_ref, cg_ref,
                  cbeta_ref, yc_ref, q_ref, k_ref, v_ref, buf_ref, sh_ref):
    i = pl.program_id(0)
    halo = pl.ds(0, CONV_HALO)
    tail = pl.ds(CONV_TS, CONV_HALO)

    @pl.when(i % tiles_per_seq == 0)
    def _():
        buf_ref[halo, :] = jnp.zeros((CONV_HALO, CONV_CH), F32)

    @pl.when(i % tiles_per_seq != 0)
    def _():
        buf_ref[halo, :] = buf_ref[tail, :]

    x = x_ref[...]
    r = lax.rsqrt(jnp.mean(x * x, axis=-1, keepdims=True) + EPS)
    xg = (x * g_ref[...]).astype(BF16)

    def proj(col0):
        return jnp.dot(xg, w_ref[:, col0:col0 + CONV_CH], preferred_element_type=F32) * r

    buf_ref[CONV_HALO:, :] = proj(0) * jax.nn.sigmoid(proj(CONV_CH))
    for s in range(1, SUBLANES):
        sh_ref[s - 1] = buf_ref[s:s + CONV_SH_ROWS, :]

    bias = cb_ref[...]
    gain = cg_ref[...]
    beta = cbeta_ref[...]
    w_ref_conv = cw_ref

    for c in range(CONV_TS // CONV_RC):
        r0 = c * CONV_RC
        groups = CONV_RC // SUBLANES
        halves = []
        for lanes in (slice(0, CONV_CH // 2), slice(CONV_CH // 2, CONV_CH)):
            acc_h = jnp.broadcast_to(bias[:, lanes], (groups, SUBLANES, CONV_CH // 2))
            for t in range(CONV_WIDTH):
                off = CONV_PAD + t
                phase, base = off % SUBLANES, off - off % SUBLANES
                if phase == 0:
                    rows = buf_ref[r0 + base:r0 + base + CONV_RC, lanes]
                else:
                    rows = sh_ref[phase - 1, r0 + base:r0 + base + CONV_RC, lanes]
                acc_h = acc_h + rows.reshape(groups, SUBLANES, CONV_CH // 2) * w_ref_conv[t, :, lanes][None]
            halves.append(acc_h.reshape(CONV_RC, CONV_CH // 2))
        acc = jnp.concatenate(halves, axis=1)
        mu = jnp.mean(acc, axis=-1, keepdims=True)
        cen = acc - mu
        var = jnp.mean(cen * cen, axis=-1, keepdims=True)
        y = cen * lax.rsqrt(var + EPS) * gain + beta
        yc_ref[r0:r0 + CONV_RC, :] = (y * jax.nn.sigmoid(y)).astype(yc_ref.dtype)

    def head_norm(z, gain_ref):
        groups = [slice(c * HEAD_GROUP, (c + 1) * HEAD_GROUP) for c in range(ATT_WIDTH // HEAD_GROUP)]
        return jnp.concatenate([_head_norm_group(z[:, s], gain_ref[:, s], e_ref[...]) for s in groups], axis=1)

    q_ref[...] = head_norm(proj(2 * CONV_CH), qg_ref) * (HEAD_DIM ** -0.5)
    k_ref[...] = head_norm(proj(2 * CONV_CH + ATT_WIDTH), kg_ref)
    v_ref[...] = proj(2 * CONV_CH + 2 * ATT_WIDTH)


def _mixin(x, seq_len, g, w, qg, kg, e, cw, cb, cg, cbeta):
    n = x.shape[0]
    d_in = w.shape[1]
    att = jax.ShapeDtypeStruct((n, ATT_WIDTH), F32)
    tile = lambda width: pl.BlockSpec((INP_TM, width), lambda i: (i, 0))
    const = lambda shape, **kw: pl.BlockSpec(shape, lambda i: (0,) * len(shape), **kw)
    return pl.pallas_call(
        functools.partial(_mixin_kernel, seq_len // INP_TM),
        grid=(n // INP_TM,),
        in_specs=[
            tile(D_MODEL),
            const((1, D_MODEL)),
            const((D_MODEL, d_in), pipeline_mode=pl.Buffered(1)),
            const((1, ATT_WIDTH)),
            const((1, ATT_WIDTH)),
            const((HEAD_GROUP, HEAD_GROUP)),
            const((CONV_WIDTH, SUBLANES, CONV_CH)),
            const((1, CONV_CH)), const((1, CONV_CH)), const((1, CONV_CH)),
        ],
        out_specs=[tile(CONV_CH), tile(ATT_WIDTH), tile(ATT_WIDTH), tile(ATT_WIDTH)],
        out_shape=[jax.ShapeDtypeStruct((n, CONV_CH), BF16), att, att, att],
        scratch_shapes=[
            pltpu.VMEM((CONV_TS + CONV_HALO, CONV_CH), F32),
            pltpu.VMEM((SUBLANES - 1, CONV_SH_ROWS, CONV_CH), F32),
        ],
        compiler_params=pltpu.CompilerParams(
            dimension_semantics=("arbitrary",),
            vmem_limit_bytes=VMEM_LIMIT),
        name="mix_in",
    )(x, g, w, qg, kg, e, cw, cb, cg, cbeta)


ATT_CH = 2048
ATT_QB = WIN
ATT_KB = 2 * WIN
ATT_TILES = ATT_CH // ATT_QB
ATT_STAGE = 4
ATT_QUART = ATT_CH // ATT_STAGE
ATT_UNROLL = 16


def _attn_kernel(slopes_ref, q_ref, kc_ref, vc_ref, o_ref,
                 tmp, qd, kd, vd, biasbuf, xbuf, ybuf, mbuf):
    hp = pl.program_id(1)
    c = pl.program_id(2)
    first_chunk = c == 0

    lane = lax.broadcasted_iota(jnp.int32, (1, LANES), 1)
    in_a = lane < HEAD_DIM
    qi = lax.broadcasted_iota(jnp.int32, (ATT_QB, ATT_KB), 0)
    kj = lax.broadcasted_iota(jnp.int32, (ATT_QB, ATT_KB), 1)
    dist = WIN + qi - kj
    valid = (dist >= 0) & (dist <= WIN)
    valid_first = valid & (kj >= WIN)
    distf = dist.astype(F32)

    for ai, ref in enumerate((q_ref, kc_ref, vc_ref)):
        for a in range(ATT_STAGE):
            tmp[ai, a * ATT_QUART:(a + 1) * ATT_QUART, :] = ref[0, pl.ds(a, ATT_QUART, stride=ATT_STAGE), :]

    for bi, d in enumerate(DILATIONS):
        span = d * WIN
        nwin = ATT_CH // span + 1

        for hh in range(2):
            alibi = (-slopes_ref[2 * hp + hh] * d) * distf
            rows = slice(hh * ATT_QB, (hh + 1) * ATT_QB)
            biasbuf[bi, 0, rows, :] = jnp.where(valid, alibi, NEG)
            biasbuf[bi, 1, rows, :] = jnp.where(valid_first, alibi, NEG)

        def load_qkv(blk, res, d=d):
            if d == 1:
                idx = pl.ds(pl.multiple_of(blk * ATT_QB, ATT_QB), ATT_QB)
                return q_ref[0, idx, :], kc_ref[0, idx, :], vc_ref[0, idx, :]
            if d == ATT_STAGE:
                idx = pl.ds(pl.multiple_of(res * ATT_QUART + blk * ATT_QB, ATT_QB), ATT_QB)
            else:
                hi = res // ATT_STAGE
                idx = pl.ds((res - hi * ATT_STAGE) * ATT_QUART + hi, ATT_QB, stride=ATT_STAGE)
            return tmp[0, idx, :], tmp[1, idx, :], tmp[2, idx, :]

        def regroup_cur(t, carry, d=d, bi=bi, nwin=nwin, load_qkv=load_qkv):
            blk = t // d
            res = t - blk * d
            qv, kv, vv = load_qkv(blk, res)
            q0 = pl.multiple_of(t * (2 * ATT_QB), 2 * ATT_QB)
            qd[bi, pl.ds(q0, ATT_QB), :] = jnp.where(in_a, qv, 0.0).astype(BF16)
            qd[bi, pl.ds(q0 + ATT_QB, ATT_QB), :] = jnp.where(in_a, 0.0, qv).astype(BF16)
            dst = pl.ds(pl.multiple_of((res * nwin + blk + 1) * ATT_QB, ATT_QB), ATT_QB)
            kd[bi, dst, :] = kv.astype(BF16)
            vd[bi, 0, dst, :] = jnp.where(in_a, vv, 1.0).astype(BF16)
            vd[bi, 1, dst, :] = jnp.where(in_a, 1.0, vv).astype(BF16)
            return carry

        def carry_prev(res, carry, bi=bi, nwin=nwin):
            dst = pl.ds(pl.multiple_of(res * nwin * ATT_QB, ATT_QB), ATT_QB)
            src = pl.ds(pl.multiple_of((res * nwin + nwin - 1) * ATT_QB, ATT_QB), ATT_QB)
            kd[bi, dst, :] = kd[bi, src, :]
            vd[bi, 0, dst, :] = vd[bi, 0, src, :]
            vd[bi, 1, dst, :] = vd[bi, 1, src, :]
            return carry

        def zero_prev(res, carry, bi=bi, nwin=nwin):
            dst = pl.ds(pl.multiple_of(res * nwin * ATT_QB, ATT_QB), ATT_QB)
            zeros = jnp.zeros((ATT_QB, LANES), BF16)
            kd[bi, dst, :] = zeros
            vd[bi, 0, dst, :] = zeros
            vd[bi, 1, dst, :] = zeros
            return carry

        @pl.when(first_chunk)
        def _(d=d, zero_prev=zero_prev):
            lax.fori_loop(0, d, zero_prev, 0)

        @pl.when(jnp.logical_not(first_chunk))
        def _(d=d, carry_prev=carry_prev):
            lax.fori_loop(0, d, carry_prev, 0)

        lax.fori_loop(0, ATT_TILES, regroup_cur, 0, unroll=2)

        def tile(t, carry, d=d, bi=bi, span=span, nwin=nwin):
            blk = t // d
            res = t - blk * d
            q0 = pl.multiple_of(t * (2 * ATT_QB), 2 * ATT_QB)
            k0 = pl.multiple_of((res * nwin + blk) * ATT_QB, ATT_QB)
            first = jnp.logical_and(first_chunk, blk == 0).astype(jnp.int32)
            s = lax.dot_general(qd[bi, pl.ds(q0, 2 * ATT_QB), :], kd[bi, pl.ds(k0, ATT_KB), :],
                                (((1,), (1,)), ((), ())), preferred_element_type=F32)
            s = s + biasbuf[bi, first]
            m = jnp.max(s, axis=-1, keepdims=True)
            p = jnp.exp(s - m).astype(BF16)
            oa = jnp.dot(p[:ATT_QB], vd[bi, 0, pl.ds(k0, ATT_KB), :], preferred_element_type=F32)
            ob = jnp.dot(p[ATT_QB:], vd[bi, 1, pl.ds(k0, ATT_KB), :], preferred_element_type=F32)
            dst = pl.ds(blk * span + res, ATT_QB, stride=d)
            xbuf[bi, dst, :] = jnp.where(in_a, oa, ob)
            ybuf[bi, dst, :] = jnp.where(in_a, ob, oa)
            ma = jnp.broadcast_to(m[:ATT_QB], (ATT_QB, LANES))
            mb = jnp.broadcast_to(m[ATT_QB:], (ATT_QB, LANES))
            mbuf[bi, 0, dst, :] = jnp.where(in_a, ma, mb)
            mbuf[bi, 1, dst, :] = jnp.where(in_a, mb, ma)
            return carry

        lax.fori_loop(0, ATT_TILES, tile, 0, unroll=ATT_UNROLL)

    def combine(t, carry):
        rows = pl.ds(pl.multiple_of(t * ATT_QB, ATT_QB), ATT_QB)
        def weighted_sum(which, vals):
            ms = [mbuf[bi, which, rows, :] for bi in range(len(DILATIONS))]
            mm = functools.reduce(jnp.maximum, ms)
            return sum(jnp.exp(mb - mm) * vals[bi, rows, :] for bi, mb in enumerate(ms))

        num = weighted_sum(0, xbuf)
        den = weighted_sum(1, ybuf)
        o_ref[0, rows, :] = (num / pltpu.roll(den, HEAD_DIM, axis=1)).astype(o_ref.dtype)
        return carry

    lax.fori_loop(0, ATT_TILES, combine, 0, unroll=2)


def _attn(q, k, v, slopes):
    bsz, s, _ = q.shape
    blk = (1, ATT_CH, LANES)
    nb = len(DILATIONS)
    cur = pl.BlockSpec(blk, lambda b, h, c, sl: (b, c, h))
    grid_spec = pltpu.PrefetchScalarGridSpec(
        num_scalar_prefetch=1,
        grid=(bsz, ATT_WIDTH // LANES, s // ATT_CH),
        in_specs=[cur, cur, cur],
        out_specs=cur,
        scratch_shapes=[
            pltpu.VMEM((3, ATT_CH, LANES), F32),
            pltpu.VMEM((nb, 2 * ATT_CH, LANES), BF16),
            pltpu.VMEM((nb, 2 * ATT_CH, LANES), BF16),
            pltpu.VMEM((nb, 2, 2 * ATT_CH, LANES), BF16),
            pltpu.VMEM((nb, 2, 2 * ATT_QB, ATT_KB), F32),
            pltpu.VMEM((nb, ATT_CH, LANES), F32),
            pltpu.VMEM((nb, ATT_CH, LANES), F32),
            pltpu.VMEM((nb, 2, ATT_CH, LANES), F32),
        ],
    )
    return pl.pallas_call(
        _attn_kernel,
        grid_spec=grid_spec,
        out_shape=jax.ShapeDtypeStruct((bsz, s, ATT_WIDTH), BF16),
        compiler_params=pltpu.CompilerParams(
            dimension_semantics=("parallel", "arbitrary", "arbitrary"),
            vmem_limit_bytes=VMEM_LIMIT),
        name="attn",
    )(slopes, q, k, v)


OUT_TM = 512


def _outproj_kernel(x_ref, yc_ref, ya_ref, wc_ref, wa_ref, o_ref):
    o_ref[...] = (x_ref[...]
                  + jnp.dot(yc_ref[...], wc_ref[...], preferred_element_type=F32)
                  + jnp.dot(ya_ref[...], wa_ref[...], preferred_element_type=F32))


def _outproj(x, yc, ya, wc, wa):
    n = x.shape[0]
    return pl.pallas_call(
        _outproj_kernel,
        grid=(n // OUT_TM,),
        in_specs=[
            pl.BlockSpec((OUT_TM, D_MODEL), lambda i: (i, 0)),
            pl.BlockSpec((OUT_TM, CONV_CH), lambda i: (i, 0)),
            pl.BlockSpec((OUT_TM, ATT_WIDTH), lambda i: (i, 0)),
            pl.BlockSpec((CONV_CH, D_MODEL), lambda i: (0, 0)),
            pl.BlockSpec((ATT_WIDTH, D_MODEL), lambda i: (0, 0)),
        ],
        out_specs=pl.BlockSpec((OUT_TM, D_MODEL), lambda i: (i, 0)),
        out_shape=jax.ShapeDtypeStruct((n, D_MODEL), F32),
        compiler_params=pltpu.CompilerParams(
            dimension_semantics=("parallel",),
            vmem_limit_bytes=VMEM_LIMIT),
        name="out_proj",
    )(x, yc, ya, wc, wa)


def _layer(x2d, bsz, s, p):
    n = bsz * s
    x2d = _ffn(x2d, p["ffn1_norm_g"], p["ffn1_w_gate"], p["ffn1_w_up"], p["ffn1_w_down"])
    yc, q, k, v = _mixin(x2d, s, p["mix_norm_g"], p["w_in"], p["q_norm_g"], p["k_norm_g"], p["head_ones"],
                         p["conv_w_dw"], p["conv_b_dw"], p["conv_ln_g"], p["conv_ln_b"])
    ya =_attn(q.reshape(bsz, s, ATT_WIDTH), k.reshape(bsz, s, ATT_WIDTH), v.reshape(bsz, s, ATT_WIDTH),
               p["slopes"])
    x2d = _outproj(x2d, yc.reshape(n, CONV_CH), ya.reshape(n, ATT_WIDTH), p["w_out_conv"], p["w_out_att"])
    return _ffn(x2d, p["ffn2_norm_g"], p["ffn2_w_gate"], p["ffn2_w_up"], p["ffn2_w_down"])


def kernel(x, ffn1_norm_g, ffn1_w_gate, ffn1_w_up, ffn1_w_down, mix_norm_g, w_in, conv_w_dw, conv_b_dw,
           conv_ln_g, conv_ln_b, q_norm_g, k_norm_g, w_out, ffn2_norm_g, ffn2_w_gate, ffn2_w_up, ffn2_w_down):
    bsz, s, _ = x.shape
    depth = w_in.shape[0]
    row = lambda a: a.reshape(1, -1).astype(F32)
    head_ids = jnp.arange(HEAD_GROUP) // HEAD_DIM
    head_ones = (head_ids[:, None] == head_ids[None, :]).astype(BF16)
    slopes = 2.0 ** (-ALIBI_MAX_BIAS * jnp.arange(1, N_HEADS + 1, dtype=F32) / N_HEADS)
    x2d = x.reshape(bsz * s, D_MODEL)
    for l in range(depth):
        p = dict(
            ffn1_norm_g=row(ffn1_norm_g[l]), ffn1_w_gate=ffn1_w_gate[l].astype(BF16),
            ffn1_w_up=ffn1_w_up[l].astype(BF16), ffn1_w_down=ffn1_w_down[l].astype(BF16),
            mix_norm_g=row(mix_norm_g[l]), w_in=w_in[l].astype(BF16),
            q_norm_g=row(jnp.tile(q_norm_g[l], N_HEADS)), k_norm_g=row(jnp.tile(k_norm_g[l], N_HEADS)),
            head_ones=head_ones, slopes=slopes,
            conv_w_dw=jnp.broadcast_to(conv_w_dw[l].astype(F32)[:, None, :], (CONV_WIDTH, SUBLANES, CONV_CH)),
            conv_b_dw=row(conv_b_dw[l]),
            conv_ln_g=row(conv_ln_g[l]), conv_ln_b=row(conv_ln_b[l]),
            w_out_conv=w_out[l, :CONV_CH].astype(BF16), w_out_att=w_out[l, CONV_CH:].astype(BF16),
            ffn2_norm_g=row(ffn2_norm_g[l]), ffn2_w_gate=ffn2_w_gate[l].astype(BF16),
            ffn2_w_up=ffn2_w_up[l].astype(BF16), ffn2_w_down=ffn2_w_down[l].astype(BF16),
        )
        x2d = _layer(x2d, bsz, s, p)
    return x2d.reshape(bsz, s, D_MODEL)
```

```python
import functools

import jax
import jax.numpy as jnp
from jax import lax
from jax.experimental import pallas as pl
from jax.experimental.pallas import tpu as pltpu

D_MODEL = 2048
D_FF = 5632
CONV_CH = 1024
CONV_WIDTH = 31
N_HEADS = 16
HEAD_DIM = 64
ATT_WIDTH = N_HEADS * HEAD_DIM
DILATIONS = (1, 4, 16)
WIN = 128
ALIBI_MAX_BIAS = 8.0
EPS = 1e-6

LANES = 128
SUBLANES = 8
VMEM_LIMIT = 56 * 1024 * 1024

F32 = jnp.float32
BF16 = jnp.bfloat16
NEG = -1e30


def _rms_norm_rows(x, g):
    ms = jnp.mean(x * x, axis=-1, keepdims=True)
    return x * lax.rsqrt(ms + EPS) * g


FFN_TM = 1024
FFN_TF = 512


def _ffn_kernel(x_ref, g_ref, wg_ref, wu_ref, wd_ref, o_ref, h_ref):
    j = pl.program_id(1)

    @pl.when(j == 0)
    def _():
        x = x_ref[...]
        h_ref[...] = _rms_norm_rows(x, g_ref[...]).astype(BF16)
        o_ref[...] = x

    h = h_ref[...]
    gate = jnp.dot(h, wg_ref[...], preferred_element_type=F32)
    up = jnp.dot(h, wu_ref[...], preferred_element_type=F32)
    a = (gate * jax.nn.sigmoid(gate) * (0.5 * up)).astype(BF16)
    o_ref[...] += jnp.dot(a, wd_ref[...], preferred_element_type=F32)


def _ffn(x, g, wg, wu, wd):
    n = x.shape[0]
    return pl.pallas_call(
        _ffn_kernel,
        grid=(n // FFN_TM, D_FF // FFN_TF),
        in_specs=[
            pl.BlockSpec((FFN_TM, D_MODEL), lambda i, j: (i, 0)),
            pl.BlockSpec((1, D_MODEL), lambda i, j: (0, 0)),
            pl.BlockSpec((D_MODEL, FFN_TF), lambda i, j: (0, j)),
            pl.BlockSpec((D_MODEL, FFN_TF), lambda i, j: (0, j)),
            pl.BlockSpec((FFN_TF, D_MODEL), lambda i, j: (j, 0)),
        ],
        out_specs=pl.BlockSpec((FFN_TM, D_MODEL), lambda i, j: (i, 0)),
        out_shape=jax.ShapeDtypeStruct((n, D_MODEL), F32),
        scratch_shapes=[pltpu.VMEM((FFN_TM, D_MODEL), BF16)],
        compiler_params=pltpu.CompilerParams(
            dimension_semantics=("parallel", "arbitrary"),
            vmem_limit_bytes=VMEM_LIMIT),
        name="ffn",
    )(x, g, wg, wu, wd)


INP_TM = 256
HEAD_GROUP = 256


def _head_norm_group(z, gain, e):
    ss = z * z
    hi = ss.astype(BF16)
    lo = (ss - hi.astype(F32)).astype(BF16)
    tot = jnp.dot(hi, e, preferred_element_type=F32) + jnp.dot(lo, e, preferred_element_type=F32)
    return z * lax.rsqrt(tot * (1.0 / HEAD_DIM) + EPS) * gain


CONV_TS = INP_TM
CONV_HALO = 32
CONV_RC = 32
CONV_PAD = CONV_HALO - (CONV_WIDTH - 1)
CONV_SH_ROWS = CONV_TS + CONV_HALO - SUBLANES


def _mixin_kernel(tiles_per_seq, x_ref, g_ref, w_ref, qg_ref, kg_ref, e_ref, cw_ref, cb_ref, cg_ref,
                  cbeta_ref, yc_ref, q_ref, k_ref, v_ref, buf_ref, sh_ref):
    i = pl.program_id(0)
    halo = pl.ds(0, CONV_HALO)
    tail = pl.ds(CONV_TS, CONV_HALO)

    @pl.when(i % tiles_per_seq == 0)
    def _():
        buf_ref[halo, :] = jnp.zeros((CONV_HALO, CONV_CH), F32)

    @pl.when(i % tiles_per_seq != 0)
    def _():
        buf_ref[halo, :] = buf_ref[tail, :]

    x = x_ref[...]
    r = lax.rsqrt(jnp.mean(x * x, axis=-1, keepdims=True) + EPS)
    xg = (x * g_ref[...]).astype(BF16)

    def proj(col0):
        return jnp.dot(xg, w_ref[:, col0:col0 + CONV_CH], preferred_element_type=F32) * r

    buf_ref[CONV_HALO:, :] = proj(0) * jax.nn.sigmoid(proj(CONV_CH))
    for s in range(1, SUBLANES):
        sh_ref[s - 1] = buf_ref[s:s + CONV_SH_ROWS, :]

    bias = cb_ref[...]
    gain = cg_ref[...]
    beta = cbeta_ref[...]
    w_ref_conv = cw_ref

    for c in range(CONV_TS // CONV_RC):
        r0 = c * CONV_RC
        groups = CONV_RC // SUBLANES
        halves = []
        for lanes in (slice(0, CONV_CH // 2), slice(CONV_CH // 2, CONV_CH)):
            acc_h = jnp.broadcast_to(bias[:, lanes], (groups, SUBLANES, CONV_CH // 2))
            for t in range(CONV_WIDTH):
                off = CONV_PAD + t
                phase, base = off % SUBLANES, off - off % SUBLANES
                if phase == 0:
                    rows = buf_ref[r0 + base:r0 + base + CONV_RC, lanes]
                else:
                    rows = sh_ref[phase - 1, r0 + base:r0 + base + CONV_RC, lanes]
                acc_h = acc_h + rows.reshape(groups, SUBLANES, CONV_CH // 2) * w_ref_conv[t, :, lanes][None]
            halves.append(acc_h.reshape(CONV_RC, CONV_CH // 2))
        acc = jnp.concatenate(halves, axis=1)
        mu = jnp.mean(acc, axis=-1, keepdims=True)
        cen = acc - mu
        var = jnp.mean(cen * cen, axis=-1, keepdims=True)
        y = cen * lax.rsqrt(var + EPS) * gain + beta
        yc_ref[r0:r0 + CONV_RC, :] = (y * jax.nn.sigmoid(y)).astype(yc_ref.dtype)

    def head_norm(z, gain_ref):
        groups = [slice(c * HEAD_GROUP, (c + 1) * HEAD_GROUP) for c in range(ATT_WIDTH // HEAD_GROUP)]
        return jnp.concatenate([_head_norm_group(z[:, s], gain_ref[:, s], e_ref[...]) for s in groups], axis=1)

    q_ref[...] = head_norm(proj(2 * CONV_CH), qg_ref) * (HEAD_DIM ** -0.5)
    k_ref[...] = head_norm(proj(2 * CONV_CH + ATT_WIDTH), kg_ref)
    v_ref[...] = proj(2 * CONV_CH + 2 * ATT_WIDTH)


def _mixin(x, seq_len, g, w, qg, kg, e, cw, cb, cg, cbeta):
    n = x.shape[0]
    d_in = w.shape[1]
    att = jax.ShapeDtypeStruct((n, ATT_WIDTH), F32)
    tile = lambda width: pl.BlockSpec((INP_TM, width), lambda i: (i, 0))
    const = lambda shape, **kw: pl.BlockSpec(shape, lambda i: (0,) * len(shape), **kw)
    return pl.pallas_call(
        functools.partial(_mixin_kernel, seq_len // INP_TM),
        grid=(n // INP_TM,),
        in_specs=[
            tile(D_MODEL),
            const((1, D_MODEL)),
            const((D_MODEL, d_in), pipeline_mode=pl.Buffered(1)),
            const((1, ATT_WIDTH)),
            const((1, ATT_WIDTH)),
            const((HEAD_GROUP, HEAD_GROUP)),
            const((CONV_WIDTH, SUBLANES, CONV_CH)),
            const((1, CONV_CH)), const((1, CONV_CH)), const((1, CONV_CH)),
        ],
        out_specs=[tile(CONV_CH), tile(ATT_WIDTH), tile(ATT_WIDTH), tile(ATT_WIDTH)],
        out_shape=[jax.ShapeDtypeStruct((n, CONV_CH), BF16), att, att, att],
        scratch_shapes=[
            pltpu.VMEM((CONV_TS + CONV_HALO, CONV_CH), F32),
            pltpu.VMEM((SUBLANES - 1, CONV_SH_ROWS, CONV_CH), F32),
        ],
        compiler_params=pltpu.CompilerParams(
            dimension_semantics=("arbitrary",),
            vmem_limit_bytes=VMEM_LIMIT),
        name="mix_in",
    )(x, g, w, qg, kg, e, cw, cb, cg, cbeta)


ATT_CH = 2048
ATT_QB = WIN
ATT_KB = 2 * WIN
ATT_TILES = ATT_CH // ATT_QB
ATT_STAGE = 4
ATT_QUART = ATT_CH // ATT_STAGE
ATT_SEG = ATT_QB // ATT_STAGE
ATT_UNROLL = 16


def _attn_kernel(slopes_ref, q_ref, kc_ref, vc_ref, o_ref,
                 tmp, qd, kd, vd, biasbuf, xbuf, ybuf, mbuf, nat):
    hp = pl.program_id(1)
    c = pl.program_id(2)
    first_chunk = c == 0

    lane = lax.broadcasted_iota(jnp.int32, (1, LANES), 1)
    in_a = lane < HEAD_DIM
    qi = lax.broadcasted_iota(jnp.int32, (ATT_QB, ATT_KB), 0)
    kj = lax.broadcasted_iota(jnp.int32, (ATT_QB, ATT_KB), 1)

    for ai, ref in enumerate((q_ref, kc_ref, vc_ref)):
        for a in range(ATT_STAGE):
            tmp[ai, a * ATT_QUART:(a + 1) * ATT_QUART, :] = ref[0, pl.ds(a, ATT_QUART, stride=ATT_STAGE), :]

    for bi, d in enumerate(DILATIONS):
        span = d * WIN
        nwin = ATT_CH // span + 1

        def grouped_rows(blk, res, d=d):
            if d == 1:
                return [(slice(a * ATT_SEG, (a + 1) * ATT_SEG),
                         pl.ds(pl.multiple_of(a * ATT_QUART + blk * ATT_SEG, ATT_SEG), ATT_SEG))
                        for a in range(ATT_STAGE)]
            if d == ATT_STAGE:
                return [(slice(None), pl.ds(pl.multiple_of(res * ATT_QUART + blk * ATT_QB, ATT_QB), ATT_QB))]
            hi = res // ATT_STAGE
            return [(slice(None), pl.ds((res - hi * ATT_STAGE) * ATT_QUART + hi, ATT_QB, stride=ATT_STAGE))]

        def step_of(idx, d=d):
            if d == 1:
                return ATT_STAGE * (idx % ATT_SEG) + idx // ATT_SEG
            return idx

        dist = WIN + step_of(qi) - (step_of(kj % WIN) + WIN * (kj // WIN))
        valid = (dist >= 0) & (dist <= WIN)
        valid_first = valid & (kj >= WIN)
        distf = dist.astype(F32)
        for hh in range(2):
            alibi = (-slopes_ref[2 * hp + hh] * d) * distf
            rows = slice(hh * ATT_QB, (hh + 1) * ATT_QB)
            biasbuf[bi, 0, rows, :] = jnp.where(valid, alibi, NEG)
            biasbuf[bi, 1, rows, :] = jnp.where(valid_first, alibi, NEG)

        def regroup_cur(t, carry, d=d, bi=bi, nwin=nwin, grouped_rows=grouped_rows):
            blk = t // d
            res = t - blk * d
            runs = grouped_rows(blk, res)
            qv, kv, vv = (jnp.concatenate([tmp[ai, idx, :] for _, idx in runs], axis=0) for ai in range(3))
            q0 = pl.multiple_of(t * (2 * ATT_QB), 2 * ATT_QB)
            qd[bi, pl.ds(q0, ATT_QB), :] = jnp.where(in_a, qv, 0.0).astype(BF16)
            qd[bi, pl.ds(q0 + ATT_QB, ATT_QB), :] = jnp.where(in_a, 0.0, qv).astype(BF16)
            dst = pl.ds(pl.multiple_of((res * nwin + blk + 1) * ATT_QB, ATT_QB), ATT_QB)
            kd[bi, dst, :] = kv.astype(BF16)
            vd[bi, 0, dst, :] = jnp.where(in_a, vv, 1.0).astype(BF16)
            vd[bi, 1, dst, :] = jnp.where(in_a, 1.0, vv).astype(BF16)
            return carry

        def carry_prev(res, carry, bi=bi, nwin=nwin):
            dst = pl.ds(pl.multiple_of(res * nwin * ATT_QB, ATT_QB), ATT_QB)
            src = pl.ds(pl.multiple_of((res * nwin + nwin - 1) * ATT_QB, ATT_QB), ATT_QB)
            kd[bi, dst, :] = kd[bi, src, :]
            vd[bi, 0, dst, :] = vd[bi, 0, src, :]
            vd[bi, 1, dst, :] = vd[bi, 1, src, :]
            return carry

        def zero_prev(res, carry, bi=bi, nwin=nwin):
            dst = pl.ds(pl.multiple_of(res * nwin * ATT_QB, ATT_QB), ATT_QB)
            zeros = jnp.zeros((ATT_QB, LANES), BF16)
            kd[bi, dst, :] = zeros
            vd[bi, 0, dst, :] = zeros
            vd[bi, 1, dst, :] = zeros
            return carry

        @pl.when(first_chunk)
        def _(d=d, zero_prev=zero_prev):
            lax.fori_loop(0, d, zero_prev, 0)

        @pl.when(jnp.logical_not(first_chunk))
        def _(d=d, carry_prev=carry_prev):
            lax.fori_loop(0, d, carry_prev, 0)

        lax.fori_loop(0, ATT_TILES, regroup_cur, 0, unroll=2)

        def tile(t, carry, d=d, bi=bi, nwin=nwin, grouped_rows=grouped_rows):
            blk = t // d
            res = t - blk * d
            q0 = pl.multiple_of(t * (2 * ATT_QB), 2 * ATT_QB)
            k0 = pl.multiple_of((res * nwin + blk) * ATT_QB, ATT_QB)
            first = jnp.logical_and(first_chunk, blk == 0).astype(jnp.int32)
            s = lax.dot_general(qd[bi, pl.ds(q0, 2 * ATT_QB), :], kd[bi, pl.ds(k0, ATT_KB), :],
                                (((1,), (1,)), ((), ())), preferred_element_type=F32)
            s = s + biasbuf[bi, first]
            m = jnp.max(s, axis=-1, keepdims=True)
            p = jnp.exp(s - m).astype(BF16)
            oa = jnp.dot(p[:ATT_QB], vd[bi, 0, pl.ds(k0, ATT_KB), :], preferred_element_type=F32)
            ob = jnp.dot(p[ATT_QB:], vd[bi, 1, pl.ds(k0, ATT_KB), :], preferred_element_type=F32)
            ma = jnp.broadcast_to(m[:ATT_QB], (ATT_QB, LANES))
            mb = jnp.broadcast_to(m[ATT_QB:], (ATT_QB, LANES))
            num = jnp.where(in_a, oa, ob)
            den = jnp.where(in_a, ob, oa)
            m_num = jnp.where(in_a, ma, mb)
            m_den = jnp.where(in_a, mb, ma)
            for rows, dst in grouped_rows(blk, res):
                xbuf[bi, dst, :] = num[rows]
                ybuf[bi, dst, :] = den[rows]
                mbuf[bi, 0, dst, :] = m_num[rows]
                mbuf[bi, 1, dst, :] = m_den[rows]
            return carry

        lax.fori_loop(0, ATT_TILES, tile, 0, unroll=ATT_UNROLL)

    def combine(t, carry):
        rows = pl.ds(pl.multiple_of(t * ATT_QB, ATT_QB), ATT_QB)

        def weighted_sum(which, vals):
            ms = [mbuf[bi, which, rows, :] for bi in range(len(DILATIONS))]
            mm = functools.reduce(jnp.maximum, ms)
            return sum(jnp.exp(mb - mm) * vals[bi, rows, :] for bi, mb in enumerate(ms))

        num = weighted_sum(0, xbuf)
        den = weighted_sum(1, ybuf)
        a = t // (ATT_QUART // ATT_QB)
        blk = t - a * (ATT_QUART // ATT_QB)
        nat[pl.ds(blk * (ATT_STAGE * ATT_QB) + a, ATT_QB, stride=ATT_STAGE), :] = (
            num / pltpu.roll(den, HEAD_DIM, axis=1))
        return carry

    lax.fori_loop(0, ATT_TILES, combine, 0, unroll=2)
    o_ref[0] = nat[...].astype(o_ref.dtype)


def _attn(q, k, v, slopes):
    bsz, s, _ = q.shape
    blk = (1, ATT_CH, LANES)
    nb = len(DILATIONS)
    cur = pl.BlockSpec(blk, lambda b, h, c, sl: (b, c, h))
    grid_spec = pltpu.PrefetchScalarGridSpec(
        num_scalar_prefetch=1,
        grid=(bsz, ATT_WIDTH // LANES, s // ATT_CH),
        in_specs=[cur, cur, cur],
        out_specs=cur,
        scratch_shapes=[
            pltpu.VMEM((3, ATT_CH, LANES), F32),
            pltpu.VMEM((nb, 2 * ATT_CH, LANES), BF16),
            pltpu.VMEM((nb, 2 * ATT_CH, LANES), BF16),
            pltpu.VMEM((nb, 2, 2 * ATT_CH, LANES), BF16),
            pltpu.VMEM((nb, 2, 2 * ATT_QB, ATT_KB), F32),
            pltpu.VMEM((nb, ATT_CH, LANES), F32),
            pltpu.VMEM((nb, ATT_CH, LANES), F32),
            pltpu.VMEM((nb, 2, ATT_CH, LANES), F32),
            pltpu.VMEM((ATT_CH, LANES), F32),
        ],
    )
    return pl.pallas_call(
        _attn_kernel,
        grid_spec=grid_spec,
        out_shape=jax.ShapeDtypeStruct((bsz, s, ATT_WIDTH), BF16),
        compiler_params=pltpu.CompilerParams(
            dimension_semantics=("parallel", "arbitrary", "arbitrary"),
            vmem_limit_bytes=VMEM_LIMIT),
        name="attn",
    )(slopes, q, k, v)


OUT_TM = 512


def _outproj_kernel(x_ref, yc_ref, ya_ref, wc_ref, wa_ref, o_ref):
    o_ref[...] = (x_ref[...]
                  + jnp.dot(yc_ref[...], wc_ref[...], preferred_element_type=F32)
                  + jnp.dot(ya_ref[...], wa_ref[...], preferred_element_type=F32))


def _outproj(x, yc, ya, wc, wa):
    n = x.shape[0]
    return pl.pallas_call(
        _outproj_kernel,
        grid=(n // OUT_TM,),
        in_specs=[
            pl.BlockSpec((OUT_TM, D_MODEL), lambda i: (i, 0)),
            pl.BlockSpec((OUT_TM, CONV_CH), lambda i: (i, 0)),
            pl.BlockSpec((OUT_TM, ATT_WIDTH), lambda i: (i, 0)),
            pl.BlockSpec((CONV_CH, D_MODEL), lambda i: (0, 0)),
            pl.BlockSpec((ATT_WIDTH, D_MODEL), lambda i: (0, 0)),
        ],
        out_specs=pl.BlockSpec((OUT_TM, D_MODEL), lambda i: (i, 0)),
        out_shape=jax.ShapeDtypeStruct((n, D_MODEL), F32),
        compiler_params=pltpu.CompilerParams(
            dimension_semantics=("parallel",),
            vmem_limit_bytes=VMEM_LIMIT),
        name="out_proj",
    )(x, yc, ya, wc, wa)


def _layer(x2d, bsz, s, p):
    n = bsz * s
    x2d = _ffn(x2d, p["ffn1_norm_g"], p["ffn1_w_gate"], p["ffn1_w_up"], p["ffn1_w_down"])
    yc, q, k, v = _mixin(x2d, s, p["mix_norm_g"], p["w_in"], p["q_norm_g"], p["k_norm_g"], p["head_ones"],
                         p["conv_w_dw"], p["conv_b_dw"], p["conv_ln_g"], p["conv_ln_b"])
    ya =_attn(q.reshape(bsz, s, ATT_WIDTH), k.reshape(bsz, s, ATT_WIDTH), v.reshape(bsz, s, ATT_WIDTH),
               p["slopes"])
    x2d = _outproj(x2d, yc.reshape(n, CONV_CH), ya.reshape(n, ATT_WIDTH), p["w_out_conv"], p["w_out_att"])
    return _ffn(x2d, p["ffn2_norm_g"], p["ffn2_w_gate"], p["ffn2_w_up"], p["ffn2_w_down"])


def kernel(x, ffn1_norm_g, ffn1_w_gate, ffn1_w_up, ffn1_w_down, mix_norm_g, w_in, conv_w_dw, conv_b_dw,
           conv_ln_g, conv_ln_b, q_norm_g, k_norm_g, w_out, ffn2_norm_g, ffn2_w_gate, ffn2_w_up, ffn2_w_down):
    bsz, s, _ = x.shape
    depth = w_in.shape[0]
    row = lambda a: a.reshape(1, -1).astype(F32)
    head_ids = jnp.arange(HEAD_GROUP) // HEAD_DIM
    head_ones = (head_ids[:, None] == head_ids[None, :]).astype(BF16)
    slopes = 2.0 ** (-ALIBI_MAX_BIAS * jnp.arange(1, N_HEADS + 1, dtype=F32) / N_HEADS)
    x2d = x.reshape(bsz * s, D_MODEL)
    for l in range(depth):
        p = dict(
            ffn1_norm_g=row(ffn1_norm_g[l]), ffn1_w_gate=ffn1_w_gate[l].astype(BF16),
            ffn1_w_up=ffn1_w_up[l].astype(BF16), ffn1_w_down=ffn1_w_down[l].astype(BF16),
            mix_norm_g=row(mix_norm_g[l]), w_in=w_in[l].astype(BF16),
            q_norm_g=row(jnp.tile(q_norm_g[l], N_HEADS)), k_norm_g=row(jnp.tile(k_norm_g[l], N_HEADS)),
            head_ones=head_ones, slopes=slopes,
            conv_w_dw=jnp.broadcast_to(conv_w_dw[l].astype(F32)[:, None, :], (CONV_WIDTH, SUBLANES, CONV_CH)),
            conv_b_dw=row(conv_b_dw[l]),
            conv_ln_g=row(conv_ln_g[l]), conv_ln_b=row(conv_ln_b[l]),
            w_out_conv=w_out[l, :CONV_CH].astype(BF16), w_out_att=w_out[l, CONV_CH:].astype(BF16),
            ffn2_norm_g=row(ffn2_norm_g[l]), ffn2_w_gate=ffn2_w_gate[l].astype(BF16),
            ffn2_w_up=ffn2_w_up[l].astype(BF16), ffn2_w_down=ffn2_w_down[l].astype(BF16),
        )
        x2d = _layer(x2d, bsz, s, p)
    return x2d.reshape(bsz, s, D_MODEL)
```

```python
import functools

import jax
import jax.numpy as jnp
from jax import lax
from jax.experimental import pallas as pl
from jax.experimental.pallas import tpu as pltpu

D_MODEL = 2048
D_FF = 5632
CONV_CH = 1024
CONV_WIDTH = 31
N_HEADS = 16
HEAD_DIM = 64
ATT_WIDTH = N_HEADS * HEAD_DIM
DILATIONS = (1, 4, 16)
WIN = 128
ALIBI_MAX_BIAS = 8.0
EPS = 1e-6

LANES = 128
SUBLANES = 8
VMEM_LIMIT = 56 * 1024 * 1024

F32 = jnp.float32
BF16 = jnp.bfloat16
NEG = -1e30


def _rms_norm_rows(x, g):
    ms = jnp.mean(x * x, axis=-1, keepdims=True)
    return x * lax.rsqrt(ms + EPS) * g


FFN_TM = 1024
FFN_TF = 512


def _ffn_kernel(x_ref, g_ref, wg_ref, wu_ref, wd_ref, o_ref, h_ref):
    j = pl.program_id(1)

    @pl.when(j == 0)
    def _():
        x = x_ref[...]
        h_ref[...] = _rms_norm_rows(x, g_ref[...]).astype(BF16)
        o_ref[...] = x

    h = h_ref[...]
    gate = jnp.dot(h, wg_ref[...], preferred_element_type=F32)
    up = jnp.dot(h, wu_ref[...], preferred_element_type=F32)
    a = (gate * jax.nn.sigmoid(gate) * (0.5 * up)).astype(BF16)
    o_ref[...] += jnp.dot(a, wd_ref[...], preferred_element_type=F32)


def _ffn(x, g, wg, wu, wd):
    n = x.shape[0]
    return pl.pallas_call(
        _ffn_kernel,
        grid=(n // FFN_TM, D_FF // FFN_TF),
        in_specs=[
            pl.BlockSpec((FFN_TM, D_MODEL), lambda i, j: (i, 0)),
            pl.BlockSpec((1, D_MODEL), lambda i, j: (0, 0)),
            pl.BlockSpec((D_MODEL, FFN_TF), lambda i, j: (0, j)),
            pl.BlockSpec((D_MODEL, FFN_TF), lambda i, j: (0, j)),
            pl.BlockSpec((FFN_TF, D_MODEL), lambda i, j: (j, 0)),
        ],
        out_specs=pl.BlockSpec((FFN_TM, D_MODEL), lambda i, j: (i, 0)),
        out_shape=jax.ShapeDtypeStruct((n, D_MODEL), F32),
        scratch_shapes=[pltpu.VMEM((FFN_TM, D_MODEL), BF16)],
        compiler_params=pltpu.CompilerParams(
            dimension_semantics=("parallel", "arbitrary"),
            vmem_limit_bytes=VMEM_LIMIT),
        name="ffn",
    )(x, g, wg, wu, wd)


INP_TM = 256
HEAD_GROUP = 256


def _head_norm_group(z, gain, e):
    ss = z * z
    hi = ss.astype(BF16)
    lo = (ss - hi.astype(F32)).astype(BF16)
    tot = jnp.dot(hi, e, preferred_element_type=F32) + jnp.dot(lo, e, preferred_element_type=F32)
    return z * lax.rsqrt(tot * (1.0 / HEAD_DIM) + EPS) * gain


CONV_TS = INP_TM
CONV_HALO = 32
CONV_RC = 32
CONV_PAD = CONV_HALO - (CONV_WIDTH - 1)
CONV_SH_ROWS = CONV_TS + CONV_HALO - SUBLANES


def _mixin_kernel(tiles_per_seq, x_ref, g_ref, w_ref, qg_ref, kg_ref, e_ref, cw_ref, cb_ref, cg_ref,
                  cbeta_ref, yc_ref, q_ref, k_ref, v_ref, buf_ref, sh_ref):
    i = pl.program_id(0)
    halo = pl.ds(0, CONV_HALO)
    tail = pl.ds(CONV_TS, CONV_HALO)

    @pl.when(i % tiles_per_seq == 0)
    def _():
        buf_ref[halo, :] = jnp.zeros((CONV_HALO, CONV_CH), F32)

    @pl.when(i % tiles_per_seq != 0)
    def _():
        buf_ref[halo, :] = buf_ref[tail, :]

    x = x_ref[...]
    r = lax.rsqrt(jnp.mean(x * x, axis=-1, keepdims=True) + EPS)
    xg = (x * g_ref[...]).astype(BF16)

    def proj(col0):
        return jnp.dot(xg, w_ref[:, col0:col0 + CONV_CH], preferred_element_type=F32) * r

    buf_ref[CONV_HALO:, :] = proj(0) * jax.nn.sigmoid(proj(CONV_CH))
    for s in range(1, SUBLANES):
        sh_ref[s - 1] = buf_ref[s:s + CONV_SH_ROWS, :]

    bias = cb_ref[...]
    gain = cg_ref[...]
    beta = cbeta_ref[...]
    w_ref_conv = cw_ref

    for c in range(CONV_TS // CONV_RC):
        r0 = c * CONV_RC
        groups = CONV_RC // SUBLANES
        halves = []
        for lanes in (slice(0, CONV_CH // 2), slice(CONV_CH // 2, CONV_CH)):
            acc_h = jnp.broadcast_to(bias[:, lanes], (groups, SUBLANES, CONV_CH // 2))
            for t in range(CONV_WIDTH):
                off = CONV_PAD + t
                phase, base = off % SUBLANES, off - off % SUBLANES
                if phase == 0:
                    rows = buf_ref[r0 + base:r0 + base + CONV_RC, lanes]
                else:
                    rows = sh_ref[phase - 1, r0 + base:r0 + base + CONV_RC, lanes]
                acc_h = acc_h + rows.reshape(groups, SUBLANES, CONV_CH // 2) * w_ref_conv[t, :, lanes][None]
            halves.append(acc_h.reshape(CONV_RC, CONV_CH // 2))
        acc = jnp.concatenate(halves, axis=1)
        mu = jnp.mean(acc, axis=-1, keepdims=True)
        cen = acc - mu
        var = jnp.mean(cen * cen, axis=-1, keepdims=True)
        y = cen * lax.rsqrt(var + EPS) * gain + beta
        yc_ref[r0:r0 + CONV_RC, :] = (y * jax.nn.sigmoid(y)).astype(yc_ref.dtype)

    def head_norm(z, gain_ref):
        groups = [slice(c * HEAD_GROUP, (c + 1) * HEAD_GROUP) for c in range(ATT_WIDTH // HEAD_GROUP)]
        return jnp.concatenate([_head_norm_group(z[:, s], gain_ref[:, s], e_ref[...]) for s in groups], axis=1)

    q_ref[...] = head_norm(proj(2 * CONV_CH), qg_ref) * (HEAD_DIM ** -0.5)
    k_ref[...] = head_norm(proj(2 * CONV_CH + ATT_WIDTH), kg_ref)
    v_ref[...] = proj(2 * CONV_CH + 2 * ATT_WIDTH)


def _mixin(x, seq_len, g, w, qg, kg, e, cw, cb, cg, cbeta):
    n = x.shape[0]
    d_in = w.shape[1]
    att = jax.ShapeDtypeStruct((n, ATT_WIDTH), F32)
    tile = lambda width: pl.BlockSpec((INP_TM, width), lambda i: (i, 0))
    const = lambda shape, **kw: pl.BlockSpec(shape, lambda i: (0,) * len(shape), **kw)
    return pl.pallas_call(
        functools.partial(_mixin_kernel, seq_len // INP_TM),
        grid=(n // INP_TM,),
        in_specs=[
            tile(D_MODEL),
            const((1, D_MODEL)),
            const((D_MODEL, d_in), pipeline_mode=pl.Buffered(1)),
            const((1, ATT_WIDTH)),
            const((1, ATT_WIDTH)),
            const((HEAD_GROUP, HEAD_GROUP)),
            const((CONV_WIDTH, SUBLANES, CONV_CH)),
            const((1, CONV_CH)), const((1, CONV_CH)), const((1, CONV_CH)),
        ],
        out_specs=[tile(CONV_CH), tile(ATT_WIDTH), tile(ATT_WIDTH), tile(ATT_WIDTH)],
        out_shape=[jax.ShapeDtypeStruct((n, CONV_CH), BF16), att, att, att],
        scratch_shapes=[
            pltpu.VMEM((CONV_TS + CONV_HALO, CONV_CH), F32),
            pltpu.VMEM((SUBLANES - 1, CONV_SH_ROWS, CONV_CH), F32),
        ],
        compiler_params=pltpu.CompilerParams(
            dimension_semantics=("arbitrary",),
            vmem_limit_bytes=VMEM_LIMIT),
        name="mix_in",
    )(x, g, w, qg, kg, e, cw, cb, cg, cbeta)


ATT_CH = 2048
ATT_QB = WIN
ATT_KB = 2 * WIN
ATT_TILES = ATT_CH // ATT_QB
ATT_STAGE = 4
ATT_QUART = ATT_CH // ATT_STAGE
ATT_SEG = ATT_QB // ATT_STAGE
ATT_UNROLL = 16


def _attn_kernel(slopes_ref, q_ref, kc_ref, vc_ref, o_ref,
                 tmp, qd, kd, vd, biasbuf, xbuf, ybuf, mbuf, nat):
    hp = pl.program_id(1)
    c = pl.program_id(2)
    first_chunk = c == 0

    lane = lax.broadcasted_iota(jnp.int32, (1, LANES), 1)
    in_a = lane < HEAD_DIM
    qi = lax.broadcasted_iota(jnp.int32, (ATT_QB, ATT_KB), 0)
    kj = lax.broadcasted_iota(jnp.int32, (ATT_QB, ATT_KB), 1)

    for ai, ref in enumerate((q_ref, kc_ref, vc_ref)):
        for a in range(ATT_STAGE):
            tmp[ai, a * ATT_QUART:(a + 1) * ATT_QUART, :] = ref[0, pl.ds(a, ATT_QUART, stride=ATT_STAGE), :]

    slot = {d: n for n, d in enumerate(d for d in DILATIONS if d != 1)}
    nwin_of = lambda d: ATT_CH // (d * WIN) + 1

    def grouped_rows(d, blk, res):
        if d == 1:
            return [(slice(a * ATT_SEG, (a + 1) * ATT_SEG),
                     pl.ds(pl.multiple_of(a * ATT_QUART + blk * ATT_SEG, ATT_SEG), ATT_SEG))
                    for a in range(ATT_STAGE)]
        if d == ATT_STAGE:
            return [(slice(None), pl.ds(pl.multiple_of(res * ATT_QUART + blk * ATT_QB, ATT_QB), ATT_QB))]
        hi = res // ATT_STAGE
        return [(slice(None), pl.ds((res - hi * ATT_STAGE) * ATT_QUART + hi, ATT_QB, stride=ATT_STAGE))]

    for d, sb in slot.items():
        nwin = nwin_of(d)

        def regroup_cur(t, carry, d=d, sb=sb, nwin=nwin):
            blk = t // d
            res = t - blk * d
            (_, idx), = grouped_rows(d, blk, res)
            qv, kv, vv = tmp[0, idx, :], tmp[1, idx, :], tmp[2, idx, :]
            q0 = pl.multiple_of(t * (2 * ATT_QB), 2 * ATT_QB)
            qd[sb, pl.ds(q0, ATT_QB), :] = jnp.where(in_a, qv, 0.0).astype(BF16)
            qd[sb, pl.ds(q0 + ATT_QB, ATT_QB), :] = jnp.where(in_a, 0.0, qv).astype(BF16)
            dst = pl.ds(pl.multiple_of((res * nwin + blk + 1) * ATT_QB, ATT_QB), ATT_QB)
            kd[sb, dst, :] = kv.astype(BF16)
            vd[sb, 0, dst, :] = jnp.where(in_a, vv, 1.0).astype(BF16)
            vd[sb, 1, dst, :] = jnp.where(in_a, 1.0, vv).astype(BF16)
            return carry

        def carry_prev(res, carry, sb=sb, nwin=nwin):
            dst = pl.ds(pl.multiple_of(res * nwin * ATT_QB, ATT_QB), ATT_QB)
            src = pl.ds(pl.multiple_of((res * nwin + nwin - 1) * ATT_QB, ATT_QB), ATT_QB)
            kd[sb, dst, :] = kd[sb, src, :]
            vd[sb, 0, dst, :] = vd[sb, 0, src, :]
            vd[sb, 1, dst, :] = vd[sb, 1, src, :]
            return carry

        def zero_prev(res, carry, sb=sb, nwin=nwin):
            dst = pl.ds(pl.multiple_of(res * nwin * ATT_QB, ATT_QB), ATT_QB)
            zeros = jnp.zeros((ATT_QB, LANES), BF16)
            kd[sb, dst, :] = zeros
            vd[sb, 0, dst, :] = zeros
            vd[sb, 1, dst, :] = zeros
            return carry

        @pl.when(first_chunk)
        def _(d=d, zero_prev=zero_prev):
            lax.fori_loop(0, d, zero_prev, 0)

        @pl.when(jnp.logical_not(first_chunk))
        def _(d=d, carry_prev=carry_prev):
            lax.fori_loop(0, d, carry_prev, 0)

        lax.fori_loop(0, ATT_TILES, regroup_cur, 0, unroll=2)

    for bi, d in enumerate(DILATIONS):
        def step_of(idx, d=d):
            if d == 1:
                return ATT_STAGE * (idx % ATT_SEG) + idx // ATT_SEG
            return idx

        dist = WIN + step_of(qi) - (step_of(kj % WIN) + WIN * (kj // WIN))
        valid = (dist >= 0) & (dist <= WIN)
        valid_first = valid & (kj >= WIN)
        distf = dist.astype(F32)
        for hh in range(2):
            alibi = (-slopes_ref[2 * hp + hh] * d) * distf
            rows = slice(hh * ATT_QB, (hh + 1) * ATT_QB)
            biasbuf[bi, 0, rows, :] = jnp.where(valid, alibi, NEG)
            biasbuf[bi, 1, rows, :] = jnp.where(valid_first, alibi, NEG)

        def operands(blk, res, t, d=d):
            if d != 1:
                sb, nwin = slot[d], nwin_of(d)
                q_rows = pl.ds(pl.multiple_of(t * (2 * ATT_QB), 2 * ATT_QB), 2 * ATT_QB)
                k_rows = pl.ds(pl.multiple_of((res * nwin + blk) * ATT_QB, ATT_QB), ATT_KB)
                return qd[sb, q_rows, :], kd[sb, k_rows, :], vd[sb, 0, k_rows, :], vd[sb, 1, k_rows, :]
            sb, nwin = slot[ATT_STAGE], nwin_of(ATT_STAGE)
            own = blk // ATT_STAGE
            part = blk - own * ATT_STAGE
            run = lambda start: pl.ds(pl.multiple_of(start, ATT_SEG), ATT_SEG)
            q_runs = [run(((own * ATT_STAGE + a) * 2 + hh) * ATT_QB + part * ATT_SEG)
                      for hh in range(2) for a in range(ATT_STAGE)]
            k_runs = [run((a * nwin + 1) * ATT_QB + (blk - 1 + w) * ATT_SEG)
                      for w in range(2) for a in range(ATT_STAGE)]
            gather = lambda ref_at, runs: jnp.concatenate([ref_at(r) for r in runs], axis=0)
            return (gather(lambda r: qd[sb, r, :], q_runs), gather(lambda r: kd[sb, r, :], k_runs),
                    gather(lambda r: vd[sb, 0, r, :], k_runs), gather(lambda r: vd[sb, 1, r, :], k_runs))

        def tile(t, carry, d=d, bi=bi, operands=operands):
            blk = t // d
            res = t - blk * d
            q2, k, va, vb = operands(blk, res, t)
            first = jnp.logical_and(first_chunk, blk == 0).astype(jnp.int32)
            s = lax.dot_general(q2, k, (((1,), (1,)), ((), ())), preferred_element_type=F32)
            s = s + biasbuf[bi, first]
            m = jnp.max(s, axis=-1, keepdims=True)
            p = jnp.exp(s - m).astype(BF16)
            oa = jnp.dot(p[:ATT_QB], va, preferred_element_type=F32)
            ob = jnp.dot(p[ATT_QB:], vb, preferred_element_type=F32)
            ma = jnp.broadcast_to(m[:ATT_QB], (ATT_QB, LANES))
            mb = jnp.broadcast_to(m[ATT_QB:], (ATT_QB, LANES))
            num = jnp.where(in_a, oa, ob)
            den = jnp.where(in_a, ob, oa)
            m_num = jnp.where(in_a, ma, mb)
            m_den = jnp.where(in_a, mb, ma)
            for rows, dst in grouped_rows(d, blk, res):
                xbuf[bi, dst, :] = num[rows]
                ybuf[bi, dst, :] = den[rows]
                mbuf[bi, 0, dst, :] = m_num[rows]
                mbuf[bi, 1, dst, :] = m_den[rows]
            return carry

        lax.fori_loop(0, ATT_TILES, tile, 0, unroll=ATT_UNROLL)

    def combine(t, carry):
        rows = pl.ds(pl.multiple_of(t * ATT_QB, ATT_QB), ATT_QB)

        def weighted_sum(which, vals):
            ms = [mbuf[bi, which, rows, :] for bi in range(len(DILATIONS))]
            mm = functools.reduce(jnp.maximum, ms)
            return sum(jnp.exp(mb - mm) * vals[bi, rows, :] for bi, mb in enumerate(ms))

        num = weighted_sum(0, xbuf)
        den = weighted_sum(1, ybuf)
        a = t // (ATT_QUART // ATT_QB)
        blk = t - a * (ATT_QUART // ATT_QB)
        nat[pl.ds(blk * (ATT_STAGE * ATT_QB) + a, ATT_QB, stride=ATT_STAGE), :] = (
            num / pltpu.roll(den, HEAD_DIM, axis=1))
        return carry

    lax.fori_loop(0, ATT_TILES, combine, 0, unroll=2)
    o_ref[0] = nat[...].astype(o_ref.dtype)


def _attn(q, k, v, slopes):
    bsz, s, _ = q.shape
    blk = (1, ATT_CH, LANES)
    nb = len(DILATIONS)
    cur = pl.BlockSpec(blk, lambda b, h, c, sl: (b, c, h))
    grid_spec = pltpu.PrefetchScalarGridSpec(
        num_scalar_prefetch=1,
        grid=(bsz, ATT_WIDTH // LANES, s // ATT_CH),
        in_specs=[cur, cur, cur],
        out_specs=cur,
        scratch_shapes=[
            pltpu.VMEM((3, ATT_CH, LANES), F32),
            pltpu.VMEM((nb - 1, 2 * ATT_CH, LANES), BF16),
            pltpu.VMEM((nb - 1, 2 * ATT_CH, LANES), BF16),
            pltpu.VMEM((nb - 1, 2, 2 * ATT_CH, LANES), BF16),
            pltpu.VMEM((nb, 2, 2 * ATT_QB, ATT_KB), F32),
            pltpu.VMEM((nb, ATT_CH, LANES), F32),
            pltpu.VMEM((nb, ATT_CH, LANES), F32),
            pltpu.VMEM((nb, 2, ATT_CH, LANES), F32),
            pltpu.VMEM((ATT_CH, LANES), F32),
        ],
    )
    return pl.pallas_call(
        _attn_kernel,
        grid_spec=grid_spec,
        out_shape=jax.ShapeDtypeStruct((bsz, s, ATT_WIDTH), BF16),
        compiler_params=pltpu.CompilerParams(
            dimension_semantics=("parallel", "arbitrary", "arbitrary"),
            vmem_limit_bytes=VMEM_LIMIT),
        name="attn",
    )(slopes, q, k, v)


OUT_TM = 512


def _outproj_kernel(x_ref, yc_ref, ya_ref, wc_ref, wa_ref, o_ref):
    o_ref[...] = (x_ref[...]
                  + jnp.dot(yc_ref[...], wc_ref[...], preferred_element_type=F32)
                  + jnp.dot(ya_ref[...], wa_ref[...], preferred_element_type=F32))


def _outproj(x, yc, ya, wc, wa):
    n = x.shape[0]
    return pl.pallas_call(
        _outproj_kernel,
        grid=(n // OUT_TM,),
        in_specs=[
            pl.BlockSpec((OUT_TM, D_MODEL), lambda i: (i, 0)),
            pl.BlockSpec((OUT_TM, CONV_CH), lambda i: (i, 0)),
            pl.BlockSpec((OUT_TM, ATT_WIDTH), lambda i: (i, 0)),
            pl.BlockSpec((CONV_CH, D_MODEL), lambda i: (0, 0)),
            pl.BlockSpec((ATT_WIDTH, D_MODEL), lambda i: (0, 0)),
        ],
        out_specs=pl.BlockSpec((OUT_TM, D_MODEL), lambda i: (i, 0)),
        out_shape=jax.ShapeDtypeStruct((n, D_MODEL), F32),
        compiler_params=pltpu.CompilerParams(
            dimension_semantics=("parallel",),
            vmem_limit_bytes=VMEM_LIMIT),
        name="out_proj",
    )(x, yc, ya, wc, wa)


def _layer(x2d, bsz, s, p):
    n = bsz * s
    x2d = _ffn(x2d, p["ffn1_norm_g"], p["ffn1_w_gate"], p["ffn1_w_up"], p["ffn1_w_down"])
    yc, q, k, v = _mixin(x2d, s, p["mix_norm_g"], p["w_in"], p["q_norm_g"], p["k_norm_g"], p["head_ones"],
                         p["conv_w_dw"], p["conv_b_dw"], p["conv_ln_g"], p["conv_ln_b"])
    ya =_attn(q.reshape(bsz, s, ATT_WIDTH), k.reshape(bsz, s, ATT_WIDTH), v.reshape(bsz, s, ATT_WIDTH),
               p["slopes"])
    x2d = _outproj(x2d, yc.reshape(n, CONV_CH), ya.reshape(n, ATT_WIDTH), p["w_out_conv"], p["w_out_att"])
    return _ffn(x2d, p["ffn2_norm_g"], p["ffn2_w_gate"], p["ffn2_w_up"], p["ffn2_w_down"])


def kernel(x, ffn1_norm_g, ffn1_w_gate, ffn1_w_up, ffn1_w_down, mix_norm_g, w_in, conv_w_dw, conv_b_dw,
           conv_ln_g, conv_ln_b, q_norm_g, k_norm_g, w_out, ffn2_norm_g, ffn2_w_gate, ffn2_w_up, ffn2_w_down):
    bsz, s, _ = x.shape
    depth = w_in.shape[0]
    row = lambda a: a.reshape(1, -1).astype(F32)
    head_ids = jnp.arange(HEAD_GROUP) // HEAD_DIM
    head_ones = (head_ids[:, None] == head_ids[None, :]).astype(BF16)
    slopes = 2.0 ** (-ALIBI_MAX_BIAS * jnp.arange(1, N_HEADS + 1, dtype=F32) / N_HEADS)
    x2d = x.reshape(bsz * s, D_MODEL)
    for l in range(depth):
        p = dict(
            ffn1_norm_g=row(ffn1_norm_g[l]), ffn1_w_gate=ffn1_w_gate[l].astype(BF16),
            ffn1_w_up=ffn1_w_up[l].astype(BF16), ffn1_w_down=ffn1_w_down[l].astype(BF16),
            mix_norm_g=row(mix_norm_g[l]), w_in=w_in[l].astype(BF16),
            q_norm_g=row(jnp.tile(q_norm_g[l], N_HEADS)), k_norm_g=row(jnp.tile(k_norm_g[l], N_HEADS)),
            head_ones=head_ones, slopes=slopes,
            conv_w_dw=jnp.broadcast_to(conv_w_dw[l].astype(F32)[:, None, :], (CONV_WIDTH, SUBLANES, CONV_CH)),
            conv_b_dw=row(conv_b_dw[l]),
            conv_ln_g=row(conv_ln_g[l]), conv_ln_b=row(conv_ln_b[l]),
            w_out_conv=w_out[l, :CONV_CH].astype(BF16), w_out_att=w_out[l, CONV_CH:].astype(BF16),
            ffn2_norm_g=row(ffn2_norm_g[l]), ffn2_w_gate=ffn2_w_gate[l].astype(BF16),
            ffn2_w_up=ffn2_w_up[l].astype(BF16), ffn2_w_down=ffn2_w_down[l].astype(BF16),
        )
        x2d = _layer(x2d, bsz, s, p)
    return x2d.reshape(bsz, s, D_MODEL)
```

```python
import functools

import jax
import jax.numpy as jnp
from jax import lax
from jax.experimental import pallas as pl
from jax.experimental.pallas import tpu as pltpu

D_MODEL = 2048
D_FF = 5632
CONV_CH = 1024
CONV_WIDTH = 31
N_HEADS = 16
HEAD_DIM = 64
ATT_WIDTH = N_HEADS * HEAD_DIM
DILATIONS = (1, 4, 16)
WIN = 128
ALIBI_MAX_BIAS = 8.0
EPS = 1e-6

LANES = 128
SUBLANES = 8
VMEM_LIMIT = 56 * 1024 * 1024

F32 = jnp.float32
BF16 = jnp.bfloat16
NEG = -1e30


def _rms_norm_rows(x, g):
    ms = jnp.mean(x * x, axis=-1, keepdims=True)
    return x * lax.rsqrt(ms + EPS) * g


FFN_TM = 1024
FFN_TF = 512


def _ffn_kernel(x_ref, g_ref, wg_ref, wu_ref, wd_ref, o_ref, h_ref):
    j = pl.program_id(1)

    @pl.when(j == 0)
    def _():
        x = x_ref[...]
        h_ref[...] = _rms_norm_rows(x, g_ref[...]).astype(BF16)
        o_ref[...] = x

    h = h_ref[...]
    gate = jnp.dot(h, wg_ref[...], preferred_element_type=F32)
    up = jnp.dot(h, wu_ref[...], preferred_element_type=F32)
    a = (gate * jax.nn.sigmoid(gate) * (0.5 * up)).astype(BF16)
    o_ref[...] += jnp.dot(a, wd_ref[...], preferred_element_type=F32)


def _ffn(x, g, wg, wu, wd):
    n = x.shape[0]
    return pl.pallas_call(
        _ffn_kernel,
        grid=(n // FFN_TM, D_FF // FFN_TF),
        in_specs=[
            pl.BlockSpec((FFN_TM, D_MODEL), lambda i, j: (i, 0)),
            pl.BlockSpec((1, D_MODEL), lambda i, j: (0, 0)),
            pl.BlockSpec((D_MODEL, FFN_TF), lambda i, j: (0, j)),
            pl.BlockSpec((D_MODEL, FFN_TF), lambda i, j: (0, j)),
            pl.BlockSpec((FFN_TF, D_MODEL), lambda i, j: (j, 0)),
        ],
        out_specs=pl.BlockSpec((FFN_TM, D_MODEL), lambda i, j: (i, 0)),
        out_shape=jax.ShapeDtypeStruct((n, D_MODEL), F32),
        scratch_shapes=[pltpu.VMEM((FFN_TM, D_MODEL), BF16)],
        compiler_params=pltpu.CompilerParams(
            dimension_semantics=("parallel", "arbitrary"),
            vmem_limit_bytes=VMEM_LIMIT),
        name="ffn",
    )(x, g, wg, wu, wd)


INP_TM = 256
HEAD_GROUP = 256


def _head_norm_group(z, gain, e):
    ss = z * z
    hi = ss.astype(BF16)
    lo = (ss - hi.astype(F32)).astype(BF16)
    tot = jnp.dot(hi, e, preferred_element_type=F32) + jnp.dot(lo, e, preferred_element_type=F32)
    return z * lax.rsqrt(tot * (1.0 / HEAD_DIM) + EPS) * gain


CONV_TS = INP_TM
CONV_HALO = 32
CONV_RC = 64
CONV_PARTS = 4
CONV_LN_ROWS = 32
CONV_PAD = CONV_HALO - (CONV_WIDTH - 1)
CONV_SH_ROWS = CONV_TS + CONV_HALO - SUBLANES


def _mixin_kernel(tiles_per_seq, x_ref, g_ref, w_ref, qg_ref, kg_ref, e_ref, cw_ref, cb_ref, cg_ref,
                  cbeta_ref, yc_ref, q_ref, k_ref, v_ref, buf_ref, sh_ref, pre_ref):
    i = pl.program_id(0)
    halo = pl.ds(0, CONV_HALO)
    tail = pl.ds(CONV_TS, CONV_HALO)

    @pl.when(i % tiles_per_seq == 0)
    def _():
        buf_ref[halo, :] = jnp.zeros((CONV_HALO, CONV_CH), F32)

    @pl.when(i % tiles_per_seq != 0)
    def _():
        buf_ref[halo, :] = buf_ref[tail, :]

    x = x_ref[...]
    r = lax.rsqrt(jnp.mean(x * x, axis=-1, keepdims=True) + EPS)
    xg = (x * g_ref[...]).astype(BF16)

    def proj(col0, width):
        return jnp.dot(xg, w_ref[:, col0:col0 + width], preferred_element_type=F32) * r

    bias = cb_ref[...]
    gain = cg_ref[...]
    beta = cbeta_ref[...]
    part = CONV_CH // CONV_PARTS
    groups = CONV_RC // SUBLANES

    def head_norm(z, gain_ref):
        groups = [slice(c * HEAD_GROUP, (c + 1) * HEAD_GROUP) for c in range(ATT_WIDTH // HEAD_GROUP)]
        return jnp.concatenate([_head_norm_group(z[:, s], gain_ref[:, s], e_ref[...]) for s in groups], axis=1)

    def project_q():
        q_ref[...] = head_norm(proj(2 * CONV_CH, ATT_WIDTH), qg_ref) * (HEAD_DIM ** -0.5)
        return q_ref

    def project_k():
        k_ref[...] = head_norm(proj(2 * CONV_CH + ATT_WIDTH, ATT_WIDTH), kg_ref)
        return k_ref

    def project_v():
        v_ref[...] = proj(2 * CONV_CH + 2 * ATT_WIDTH, ATT_WIDTH)
        return v_ref

    never = i < 0
    att_projections = [project_q, project_k, project_v]

    for n, lanes in enumerate(slice(n * part, (n + 1) * part) for n in range(CONV_PARTS)):
        glu = proj(lanes.start, part) * jax.nn.sigmoid(proj(CONV_CH + lanes.start, part))
        if n > 0 and att_projections:
            stored = att_projections.pop(0)()
            glu = glu + jnp.where(never, stored[0:1, 0:part], 0.0)
        buf_ref[CONV_HALO:, lanes] = glu
        for s in range(1, SUBLANES):
            sh_ref[s - 1, :, lanes] = buf_ref[s:s + CONV_SH_ROWS, lanes]
        for c in range(CONV_TS // CONV_RC):
            r0 = c * CONV_RC
            acc = jnp.broadcast_to(bias[:, lanes], (groups, SUBLANES, part))
            for t in range(CONV_WIDTH):
                off = CONV_PAD + t
                phase, base = off % SUBLANES, off - off % SUBLANES
                if phase == 0:
                    rows = buf_ref[r0 + base:r0 + base + CONV_RC, lanes]
                else:
                    rows = sh_ref[phase - 1, r0 + base:r0 + base + CONV_RC, lanes]
                acc = acc + rows.reshape(groups, SUBLANES, part) * cw_ref[t, :, lanes][None]
            pre_ref[r0:r0 + CONV_RC, lanes] = acc.reshape(CONV_RC, part)
    for project in att_projections:
        project()

    for r0 in range(0, CONV_TS, CONV_LN_ROWS):
        acc = pre_ref[r0:r0 + CONV_LN_ROWS, :]
        mu = jnp.mean(acc, axis=-1, keepdims=True)
        cen = acc - mu
        var = jnp.mean(cen * cen, axis=-1, keepdims=True)
        y = cen * lax.rsqrt(var + EPS) * gain + beta
        yc_ref[r0:r0 + CONV_LN_ROWS, :] = (y * jax.nn.sigmoid(y)).astype(yc_ref.dtype)


def _mixin(x, seq_len, g, w, qg, kg, e, cw, cb, cg, cbeta):
    n = x.shape[0]
    d_in = w.shape[1]
    att = jax.ShapeDtypeStruct((n, ATT_WIDTH), F32)
    tile = lambda width: pl.BlockSpec((INP_TM, width), lambda i: (i, 0))
    const = lambda shape, **kw: pl.BlockSpec(shape, lambda i: (0,) * len(shape), **kw)
    return pl.pallas_call(
        functools.partial(_mixin_kernel, seq_len // INP_TM),
        grid=(n // INP_TM,),
        in_specs=[
            tile(D_MODEL),
            const((1, D_MODEL)),
            const((D_MODEL, d_in), pipeline_mode=pl.Buffered(1)),
            const((1, ATT_WIDTH)),
            const((1, ATT_WIDTH)),
            const((HEAD_GROUP, HEAD_GROUP)),
            const((CONV_WIDTH, SUBLANES, CONV_CH)),
            const((1, CONV_CH)), const((1, CONV_CH)), const((1, CONV_CH)),
        ],
        out_specs=[tile(CONV_CH), tile(ATT_WIDTH), tile(ATT_WIDTH), tile(ATT_WIDTH)],
        out_shape=[jax.ShapeDtypeStruct((n, CONV_CH), BF16), att, att, att],
        scratch_shapes=[
            pltpu.VMEM((CONV_TS + CONV_HALO, CONV_CH), F32),
            pltpu.VMEM((SUBLANES - 1, CONV_SH_ROWS, CONV_CH), F32),
            pltpu.VMEM((CONV_TS, CONV_CH), F32),
        ],
        compiler_params=pltpu.CompilerParams(
            dimension_semantics=("arbitrary",),
            vmem_limit_bytes=VMEM_LIMIT),
        name="mix_in",
    )(x, g, w, qg, kg, e, cw, cb, cg, cbeta)


ATT_CH = 2048
ATT_QB = WIN
ATT_KB = 2 * WIN
ATT_TILES = ATT_CH // ATT_QB
ATT_STAGE = 4
ATT_QUART = ATT_CH // ATT_STAGE
ATT_SEG = ATT_QB // ATT_STAGE
ATT_UNROLL = 16


def _attn_kernel(slopes_ref, q_ref, kc_ref, vc_ref, o_ref,
                 tmp, qd, kd, vd, biasbuf, xbuf, ybuf, mbuf, nat):
    hp = pl.program_id(1)
    c = pl.program_id(2)
    first_chunk = c == 0

    lane = lax.broadcasted_iota(jnp.int32, (1, LANES), 1)
    in_a = lane < HEAD_DIM
    qi = lax.broadcasted_iota(jnp.int32, (ATT_QB, ATT_KB), 0)
    kj = lax.broadcasted_iota(jnp.int32, (ATT_QB, ATT_KB), 1)

    for ai, ref in enumerate((q_ref, kc_ref, vc_ref)):
        for a in range(ATT_STAGE):
            tmp[ai, a * ATT_QUART:(a + 1) * ATT_QUART, :] = ref[0, pl.ds(a, ATT_QUART, stride=ATT_STAGE), :]

    slot = {d: n for n, d in enumerate(d for d in DILATIONS if d != 1)}
    nwin_of = lambda d: ATT_CH // (d * WIN) + 1

    def grouped_rows(d, blk, res):
        if d == 1:
            return [(slice(a * ATT_SEG, (a + 1) * ATT_SEG),
                     pl.ds(pl.multiple_of(a * ATT_QUART + blk * ATT_SEG, ATT_SEG), ATT_SEG))
                    for a in range(ATT_STAGE)]
        if d == ATT_STAGE:
            return [(slice(None), pl.ds(pl.multiple_of(res * ATT_QUART + blk * ATT_QB, ATT_QB), ATT_QB))]
        hi = res // ATT_STAGE
        return [(slice(None), pl.ds((res - hi * ATT_STAGE) * ATT_QUART + hi, ATT_QB, stride=ATT_STAGE))]

    for d, sb in slot.items():
        nwin = nwin_of(d)

        def regroup_cur(t, carry, d=d, sb=sb, nwin=nwin):
            blk = t // d
            res = t - blk * d
            (_, idx), = grouped_rows(d, blk, res)
            qv, kv, vv = tmp[0, idx, :], tmp[1, idx, :], tmp[2, idx, :]
            q0 = pl.multiple_of(t * (2 * ATT_QB), 2 * ATT_QB)
            qd[sb, pl.ds(q0, ATT_QB), :] = jnp.where(in_a, qv, 0.0).astype(BF16)
            qd[sb, pl.ds(q0 + ATT_QB, ATT_QB), :] = jnp.where(in_a, 0.0, qv).astype(BF16)
            dst = pl.ds(pl.multiple_of((res * nwin + blk + 1) * ATT_QB, ATT_QB), ATT_QB)
            kd[sb, dst, :] = kv.astype(BF16)
            vd[sb, 0, dst, :] = jnp.where(in_a, vv, 1.0).astype(BF16)
            vd[sb, 1, dst, :] = jnp.where(in_a, 1.0, vv).astype(BF16)
            return carry

        def carry_prev(res, carry, sb=sb, nwin=nwin):
            dst = pl.ds(pl.multiple_of(res * nwin * ATT_QB, ATT_QB), ATT_QB)
            src = pl.ds(pl.multiple_of((res * nwin + nwin - 1) * ATT_QB, ATT_QB), ATT_QB)
            kd[sb, dst, :] = kd[sb, src, :]
            vd[sb, 0, dst, :] = vd[sb, 0, src, :]
            vd[sb, 1, dst, :] = vd[sb, 1, src, :]
            return carry

        def zero_prev(res, carry, sb=sb, nwin=nwin):
            dst = pl.ds(pl.multiple_of(res * nwin * ATT_QB, ATT_QB), ATT_QB)
            zeros = jnp.zeros((ATT_QB, LANES), BF16)
            kd[sb, dst, :] = zeros
            vd[sb, 0, dst, :] = zeros
            vd[sb, 1, dst, :] = zeros
            return carry

        @pl.when(first_chunk)
        def _(d=d, zero_prev=zero_prev):
            lax.fori_loop(0, d, zero_prev, 0)

        @pl.when(jnp.logical_not(first_chunk))
        def _(d=d, carry_prev=carry_prev):
            lax.fori_loop(0, d, carry_prev, 0)

        lax.fori_loop(0, ATT_TILES, regroup_cur, 0, unroll=2)

    for bi, d in enumerate(DILATIONS):
        def step_of(idx, d=d):
            if d == 1:
                return ATT_STAGE * (idx % ATT_SEG) + idx // ATT_SEG
            return idx

        dist = WIN + step_of(qi) - (step_of(kj % WIN) + WIN * (kj // WIN))
        valid = (dist >= 0) & (dist <= WIN)
        valid_first = valid & (kj >= WIN)
        distf = dist.astype(F32)
        for hh in range(2):
            alibi = (-slopes_ref[2 * hp + hh] * d) * distf
            rows = slice(hh * ATT_QB, (hh + 1) * ATT_QB)
            biasbuf[bi, 0, rows, :] = jnp.where(valid, alibi, NEG)
            biasbuf[bi, 1, rows, :] = jnp.where(valid_first, alibi, NEG)

        def operands(blk, res, t, d=d):
            if d != 1:
                sb, nwin = slot[d], nwin_of(d)
                q_rows = pl.ds(pl.multiple_of(t * (2 * ATT_QB), 2 * ATT_QB), 2 * ATT_QB)
                k_rows = pl.ds(pl.multiple_of((res * nwin + blk) * ATT_QB, ATT_QB), ATT_KB)
                return qd[sb, q_rows, :], kd[sb, k_rows, :], vd[sb, 0, k_rows, :], vd[sb, 1, k_rows, :]
            sb, nwin = slot[ATT_STAGE], nwin_of(ATT_STAGE)
            own = blk // ATT_STAGE
            part = blk - own * ATT_STAGE
            run = lambda start: pl.ds(pl.multiple_of(start, ATT_SEG), ATT_SEG)
            q_runs = [run(((own * ATT_STAGE + a) * 2 + hh) * ATT_QB + part * ATT_SEG)
                      for hh in range(2) for a in range(ATT_STAGE)]
            k_runs = [run((a * nwin + 1) * ATT_QB + (blk - 1 + w) * ATT_SEG)
                      for w in range(2) for a in range(ATT_STAGE)]
            gather = lambda ref_at, runs: jnp.concatenate([ref_at(r) for r in runs], axis=0)
            return (gather(lambda r: qd[sb, r, :], q_runs), gather(lambda r: kd[sb, r, :], k_runs),
                    gather(lambda r: vd[sb, 0, r, :], k_runs), gather(lambda r: vd[sb, 1, r, :], k_runs))

        def tile(t, carry, d=d, bi=bi, operands=operands):
            blk = t // d
            res = t - blk * d
            q2, k, va, vb = operands(blk, res, t)
            first = jnp.logical_and(first_chunk, blk == 0).astype(jnp.int32)
            s = lax.dot_general(q2, k, (((1,), (1,)), ((), ())), preferred_element_type=F32)
            s = s + biasbuf[bi, first]
            m = jnp.max(s, axis=-1, keepdims=True)
            p = jnp.exp(s - m).astype(BF16)
            oa = jnp.dot(p[:ATT_QB], va, preferred_element_type=F32)
            ob = jnp.dot(p[ATT_QB:], vb, preferred_element_type=F32)
            ma = jnp.broadcast_to(m[:ATT_QB], (ATT_QB, LANES))
            mb = jnp.broadcast_to(m[ATT_QB:], (ATT_QB, LANES))
            num = jnp.where(in_a, oa, ob)
            den = jnp.where(in_a, ob, oa)
            m_num = jnp.where(in_a, ma, mb)
            m_den = jnp.where(in_a, mb, ma)
            for rows, dst in grouped_rows(d, blk, res):
                xbuf[bi, dst, :] = num[rows]
                ybuf[bi, dst, :] = den[rows]
                mbuf[bi, 0, dst, :] = m_num[rows]
                mbuf[bi, 1, dst, :] = m_den[rows]
            return carry

        lax.fori_loop(0, ATT_TILES, tile, 0, unroll=ATT_UNROLL)

    def combine(t, carry):
        rows = pl.ds(pl.multiple_of(t * ATT_QB, ATT_QB), ATT_QB)

        def weighted_sum(which, vals):
            ms = [mbuf[bi, which, rows, :] for bi in range(len(DILATIONS))]
            mm = functools.reduce(jnp.maximum, ms)
            return sum(jnp.exp(mb - mm) * vals[bi, rows, :] for bi, mb in enumerate(ms))

        num = weighted_sum(0, xbuf)
        den = weighted_sum(1, ybuf)
        a = t // (ATT_QUART // ATT_QB)
        blk = t - a * (ATT_QUART // ATT_QB)
        nat[pl.ds(blk * (ATT_STAGE * ATT_QB) + a, ATT_QB, stride=ATT_STAGE), :] = (
            num / pltpu.roll(den, HEAD_DIM, axis=1))
        return carry

    lax.fori_loop(0, ATT_TILES, combine, 0, unroll=2)
    o_ref[0] = nat[...].astype(o_ref.dtype)


def _attn(q, k, v, slopes):
    bsz, s, _ = q.shape
    blk = (1, ATT_CH, LANES)
    nb = len(DILATIONS)
    cur = pl.BlockSpec(blk, lambda b, h, c, sl: (b, c, h))
    grid_spec = pltpu.PrefetchScalarGridSpec(
        num_scalar_prefetch=1,
        grid=(bsz, ATT_WIDTH // LANES, s // ATT_CH),
        in_specs=[cur, cur, cur],
        out_specs=cur,
        scratch_shapes=[
            pltpu.VMEM((3, ATT_CH, LANES), F32),
            pltpu.VMEM((nb - 1, 2 * ATT_CH, LANES), BF16),
            pltpu.VMEM((nb - 1, 2 * ATT_CH, LANES), BF16),
            pltpu.VMEM((nb - 1, 2, 2 * ATT_CH, LANES), BF16),
            pltpu.VMEM((nb, 2, 2 * ATT_QB, ATT_KB), F32),
            pltpu.VMEM((nb, ATT_CH, LANES), F32),
            pltpu.VMEM((nb, ATT_CH, LANES), F32),
            pltpu.VMEM((nb, 2, ATT_CH, LANES), F32),
            pltpu.VMEM((ATT_CH, LANES), F32),
        ],
    )
    return pl.pallas_call(
        _attn_kernel,
        grid_spec=grid_spec,
        out_shape=jax.ShapeDtypeStruct((bsz, s, ATT_WIDTH), BF16),
        compiler_params=pltpu.CompilerParams(
            dimension_semantics=("parallel", "arbitrary", "arbitrary"),
            vmem_limit_bytes=VMEM_LIMIT),
        name="attn",
    )(slopes, q, k, v)


OUT_TM = 512


def _outproj_kernel(x_ref, yc_ref, ya_ref, wc_ref, wa_ref, o_ref):
    o_ref[...] = (x_ref[...]
                  + jnp.dot(yc_ref[...], wc_ref[...], preferred_element_type=F32)
                  + jnp.dot(ya_ref[...], wa_ref[...], preferred_element_type=F32))


def _outproj(x, yc, ya, wc, wa):
    n = x.shape[0]
    return pl.pallas_call(
        _outproj_kernel,
        grid=(n // OUT_TM,),
        in_specs=[
            pl.BlockSpec((OUT_TM, D_MODEL), lambda i: (i, 0)),
            pl.BlockSpec((OUT_TM, CONV_CH), lambda i: (i, 0)),
            pl.BlockSpec((OUT_TM, ATT_WIDTH), lambda i: (i, 0)),
            pl.BlockSpec((CONV_CH, D_MODEL), lambda i: (0, 0)),
            pl.BlockSpec((ATT_WIDTH, D_MODEL), lambda i: (0, 0)),
        ],
        out_specs=pl.BlockSpec((OUT_TM, D_MODEL), lambda i: (i, 0)),
        out_shape=jax.ShapeDtypeStruct((n, D_MODEL), F32),
        compiler_params=pltpu.CompilerParams(
            dimension_semantics=("parallel",),
            vmem_limit_bytes=VMEM_LIMIT),
        name="out_proj",
    )(x, yc, ya, wc, wa)


def _layer(x2d, bsz, s, p):
    n = bsz * s
    x2d = _ffn(x2d, p["ffn1_norm_g"], p["ffn1_w_gate"], p["ffn1_w_up"], p["ffn1_w_down"])
    yc, q, k, v = _mixin(x2d, s, p["mix_norm_g"], p["w_in"], p["q_norm_g"], p["k_norm_g"], p["head_ones"],
                         p["conv_w_dw"], p["conv_b_dw"], p["conv_ln_g"], p["conv_ln_b"])
    ya =_attn(q.reshape(bsz, s, ATT_WIDTH), k.reshape(bsz, s, ATT_WIDTH), v.reshape(bsz, s, ATT_WIDTH),
               p["slopes"])
    x2d = _outproj(x2d, yc.reshape(n, CONV_CH), ya.reshape(n, ATT_WIDTH), p["w_out_conv"], p["w_out_att"])
    return _ffn(x2d, p["ffn2_norm_g"], p["ffn2_w_gate"], p["ffn2_w_up"], p["ffn2_w_down"])


def kernel(x, ffn1_norm_g, ffn1_w_gate, ffn1_w_up, ffn1_w_down, mix_norm_g, w_in, conv_w_dw, conv_b_dw,
           conv_ln_g, conv_ln_b, q_norm_g, k_norm_g, w_out, ffn2_norm_g, ffn2_w_gate, ffn2_w_up, ffn2_w_down):
    bsz, s, _ = x.shape
    depth = w_in.shape[0]
    row = lambda a: a.reshape(1, -1).astype(F32)
    head_ids = jnp.arange(HEAD_GROUP) // HEAD_DIM
    head_ones = (head_ids[:, None] == head_ids[None, :]).astype(BF16)
    slopes = 2.0 ** (-ALIBI_MAX_BIAS * jnp.arange(1, N_HEADS + 1, dtype=F32) / N_HEADS)
    x2d = x.reshape(bsz * s, D_MODEL)
    for l in range(depth):
        p = dict(
            ffn1_norm_g=row(ffn1_norm_g[l]), ffn1_w_gate=ffn1_w_gate[l].astype(BF16),
            ffn1_w_up=ffn1_w_up[l].astype(BF16), ffn1_w_down=ffn1_w_down[l].astype(BF16),
            mix_norm_g=row(mix_norm_g[l]), w_in=w_in[l].astype(BF16),
            q_norm_g=row(jnp.tile(q_norm_g[l], N_HEADS)), k_norm_g=row(jnp.tile(k_norm_g[l], N_HEADS)),
            head_ones=head_ones, slopes=slopes,
            conv_w_dw=jnp.broadcast_to(conv_w_dw[l].astype(F32)[:, None, :], (CONV_WIDTH, SUBLANES, CONV_CH)),
            conv_b_dw=row(conv_b_dw[l]),
            conv_ln_g=row(conv_ln_g[l]), conv_ln_b=row(conv_ln_b[l]),
            w_out_conv=w_out[l, :CONV_CH].astype(BF16), w_out_att=w_out[l, CONV_CH:].astype(BF16),
            ffn2_norm_g=row(ffn2_norm_g[l]), ffn2_w_gate=ffn2_w_gate[l].astype(BF16),
            ffn2_w_up=ffn2_w_up[l].astype(BF16), ffn2_w_down=ffn2_w_down[l].astype(BF16),
        )
        x2d = _layer(x2d, bsz, s, p)
    return x2d.reshape(bsz, s, D_MODEL)
```

```python
import functools

import jax
import jax.numpy as jnp
from jax import lax
from jax.experimental import pallas as pl
from jax.experimental.pallas import tpu as pltpu

D_MODEL = 2048
D_FF = 5632
CONV_CH = 1024
CONV_WIDTH = 31
N_HEADS = 16
HEAD_DIM = 64
ATT_WIDTH = N_HEADS * HEAD_DIM
DILATIONS = (1, 4, 16)
WIN = 128
ALIBI_MAX_BIAS = 8.0
EPS = 1e-6

LANES = 128
SUBLANES = 8
VMEM_LIMIT = 56 * 1024 * 1024

F32 = jnp.float32
BF16 = jnp.bfloat16
NEG = -1e30


def _rms_norm_rows(x, g):
    ms = jnp.mean(x * x, axis=-1, keepdims=True)
    return x * lax.rsqrt(ms + EPS) * g


FFN_TM = 1024
FFN_TF = 512


def _ffn_kernel(x_ref, g_ref, wg_ref, wu_ref, wd_ref, o_ref, h_ref, r_ref):
    j = pl.program_id(1)

    def chunk(h, r):
        r = jnp.concatenate([r] * (FFN_TF // LANES), axis=1)
        gate = jnp.dot(h, wg_ref[...], preferred_element_type=F32) * r
        up = jnp.dot(h, wu_ref[...], preferred_element_type=F32) * r
        a = (gate * jax.nn.sigmoid(gate) * (0.5 * up)).astype(BF16)
        return jnp.dot(a, wd_ref[...], preferred_element_type=F32)

    @pl.when(j == 0)
    def _():
        x = x_ref[...]
        h = (x * g_ref[...]).astype(BF16)
        r = jnp.broadcast_to(lax.rsqrt(jnp.mean(x * x, axis=-1, keepdims=True) + EPS), (FFN_TM, LANES))
        h_ref[...] = h
        r_ref[...] = r
        o_ref[...] = x + chunk(h, r)

    @pl.when(j > 0)
    def _():
        o_ref[...] += chunk(h_ref[...], r_ref[...])


def _ffn(x, g, wg, wu, wd):
    n = x.shape[0]
    return pl.pallas_call(
        _ffn_kernel,
        grid=(n // FFN_TM, D_FF // FFN_TF),
        in_specs=[
            pl.BlockSpec((FFN_TM, D_MODEL), lambda i, j: (i, 0)),
            pl.BlockSpec((1, D_MODEL), lambda i, j: (0, 0)),
            pl.BlockSpec((D_MODEL, FFN_TF), lambda i, j: (0, j)),
            pl.BlockSpec((D_MODEL, FFN_TF), lambda i, j: (0, j)),
            pl.BlockSpec((FFN_TF, D_MODEL), lambda i, j: (j, 0)),
        ],
        out_specs=pl.BlockSpec((FFN_TM, D_MODEL), lambda i, j: (i, 0)),
        out_shape=jax.ShapeDtypeStruct((n, D_MODEL), F32),
        scratch_shapes=[pltpu.VMEM((FFN_TM, D_MODEL), BF16), pltpu.VMEM((FFN_TM, LANES), F32)],
        compiler_params=pltpu.CompilerParams(
            dimension_semantics=("parallel", "arbitrary"),
            vmem_limit_bytes=VMEM_LIMIT),
        name="ffn",
    )(x, g, wg, wu, wd)


INP_TM = 256
HEAD_GROUP = 256


def _head_norm_group(z, gain, e):
    ss = z * z
    hi = ss.astype(BF16)
    lo = (ss - hi.astype(F32)).astype(BF16)
    tot = jnp.dot(hi, e, preferred_element_type=F32) + jnp.dot(lo, e, preferred_element_type=F32)
    return z * lax.rsqrt(tot * (1.0 / HEAD_DIM) + EPS) * gain


CONV_TS = INP_TM
CONV_HALO = 32
CONV_RC = 64
CONV_PARTS = 4
CONV_LN_ROWS = 32
CONV_PAD = CONV_HALO - (CONV_WIDTH - 1)
CONV_SH_ROWS = CONV_TS + CONV_HALO - SUBLANES


def _mixin_kernel(tiles_per_seq, x_ref, g_ref, w_ref, qg_ref, kg_ref, e_ref, cw_ref, cb_ref, cg_ref,
                  cbeta_ref, yc_ref, q_ref, k_ref, v_ref, buf_ref, sh_ref, pre_ref):
    i = pl.program_id(0)
    halo = pl.ds(0, CONV_HALO)
    tail = pl.ds(CONV_TS, CONV_HALO)

    @pl.when(i % tiles_per_seq == 0)
    def _():
        buf_ref[halo, :] = jnp.zeros((CONV_HALO, CONV_CH), F32)

    @pl.when(i % tiles_per_seq != 0)
    def _():
        buf_ref[halo, :] = buf_ref[tail, :]

    x = x_ref[...]
    r = lax.rsqrt(jnp.mean(x * x, axis=-1, keepdims=True) + EPS)
    xg = (x * g_ref[...]).astype(BF16)

    def proj(col0, width):
        return jnp.dot(xg, w_ref[:, col0:col0 + width], preferred_element_type=F32) * r

    bias = cb_ref[...]
    gain = cg_ref[...]
    beta = cbeta_ref[...]
    part = CONV_CH // CONV_PARTS
    groups = CONV_RC // SUBLANES

    def head_norm(z, gain_ref):
        groups = [slice(c * HEAD_GROUP, (c + 1) * HEAD_GROUP) for c in range(ATT_WIDTH // HEAD_GROUP)]
        return jnp.concatenate([_head_norm_group(z[:, s], gain_ref[:, s], e_ref[...]) for s in groups], axis=1)

    def project_q():
        q_ref[...] = head_norm(proj(2 * CONV_CH, ATT_WIDTH), qg_ref) * (HEAD_DIM ** -0.5)
        return q_ref

    def project_k():
        k_ref[...] = head_norm(proj(2 * CONV_CH + ATT_WIDTH, ATT_WIDTH), kg_ref)
        return k_ref

    def project_v():
        v_ref[...] = proj(2 * CONV_CH + 2 * ATT_WIDTH, ATT_WIDTH)
        return v_ref

    never = i < 0
    att_projections = [project_q, project_k, project_v]

    for n, lanes in enumerate(slice(n * part, (n + 1) * part) for n in range(CONV_PARTS)):
        glu = proj(lanes.start, part) * jax.nn.sigmoid(proj(CONV_CH + lanes.start, part))
        if n > 0 and att_projections:
            stored = att_projections.pop(0)()
            glu = glu + jnp.where(never, stored[0:1, 0:part], 0.0)
        buf_ref[CONV_HALO:, lanes] = glu
        for s in range(1, SUBLANES):
            sh_ref[s - 1, :, lanes] = buf_ref[s:s + CONV_SH_ROWS, lanes]
        for c in range(CONV_TS // CONV_RC):
            r0 = c * CONV_RC
            acc = jnp.broadcast_to(bias[:, lanes], (groups, SUBLANES, part))
            for t in range(CONV_WIDTH):
                off = CONV_PAD + t
                phase, base = off % SUBLANES, off - off % SUBLANES
                if phase == 0:
                    rows = buf_ref[r0 + base:r0 + base + CONV_RC, lanes]
                else:
                    rows = sh_ref[phase - 1, r0 + base:r0 + base + CONV_RC, lanes]
                acc = acc + rows.reshape(groups, SUBLANES, part) * cw_ref[t, :, lanes][None]
            pre_ref[r0:r0 + CONV_RC, lanes] = acc.reshape(CONV_RC, part)
    for project in att_projections:
        project()

    for r0 in range(0, CONV_TS, CONV_LN_ROWS):
        acc = pre_ref[r0:r0 + CONV_LN_ROWS, :]
        mu = jnp.mean(acc, axis=-1, keepdims=True)
        cen = acc - mu
        var = jnp.mean(cen * cen, axis=-1, keepdims=True)
        y = cen * lax.rsqrt(var + EPS) * gain + beta
        yc_ref[r0:r0 + CONV_LN_ROWS, :] = (y * jax.nn.sigmoid(y)).astype(yc_ref.dtype)


def _mixin(x, seq_len, g, w, qg, kg, e, cw, cb, cg, cbeta):
    n = x.shape[0]
    d_in = w.shape[1]
    att = jax.ShapeDtypeStruct((n, ATT_WIDTH), F32)
    tile = lambda width: pl.BlockSpec((INP_TM, width), lambda i: (i, 0))
    const = lambda shape, **kw: pl.BlockSpec(shape, lambda i: (0,) * len(shape), **kw)
    return pl.pallas_call(
        functools.partial(_mixin_kernel, seq_len // INP_TM),
        grid=(n // INP_TM,),
        in_specs=[
            tile(D_MODEL),
            const((1, D_MODEL)),
            const((D_MODEL, d_in), pipeline_mode=pl.Buffered(1)),
            const((1, ATT_WIDTH)),
            const((1, ATT_WIDTH)),
            const((HEAD_GROUP, HEAD_GROUP)),
            const((CONV_WIDTH, SUBLANES, CONV_CH)),
            const((1, CONV_CH)), const((1, CONV_CH)), const((1, CONV_CH)),
        ],
        out_specs=[tile(CONV_CH), tile(ATT_WIDTH), tile(ATT_WIDTH), tile(ATT_WIDTH)],
        out_shape=[jax.ShapeDtypeStruct((n, CONV_CH), BF16), att, att, att],
        scratch_shapes=[
            pltpu.VMEM((CONV_TS + CONV_HALO, CONV_CH), F32),
            pltpu.VMEM((SUBLANES - 1, CONV_SH_ROWS, CONV_CH), F32),
            pltpu.VMEM((CONV_TS, CONV_CH), F32),
        ],
        compiler_params=pltpu.CompilerParams(
            dimension_semantics=("arbitrary",),
            vmem_limit_bytes=VMEM_LIMIT),
        name="mix_in",
    )(x, g, w, qg, kg, e, cw, cb, cg, cbeta)


ATT_CH = 2048
ATT_QB = WIN
ATT_KB = 2 * WIN
ATT_TILES = ATT_CH // ATT_QB
ATT_STAGE = 4
ATT_QUART = ATT_CH // ATT_STAGE
ATT_SEG = ATT_QB // ATT_STAGE
ATT_UNROLL = 16


def _attn_kernel(slopes_ref, q_ref, kc_ref, vc_ref, o_ref,
                 tmp, qd, kd, vd, biasbuf, xbuf, ybuf, mbuf, nat):
    hp = pl.program_id(1)
    c = pl.program_id(2)
    first_chunk = c == 0

    lane = lax.broadcasted_iota(jnp.int32, (1, LANES), 1)
    in_a = lane < HEAD_DIM
    qi = lax.broadcasted_iota(jnp.int32, (ATT_QB, ATT_KB), 0)
    kj = lax.broadcasted_iota(jnp.int32, (ATT_QB, ATT_KB), 1)

    for ai, ref in enumerate((q_ref, kc_ref, vc_ref)):
        for a in range(ATT_STAGE):
            tmp[ai, a * ATT_QUART:(a + 1) * ATT_QUART, :] = ref[0, pl.ds(a, ATT_QUART, stride=ATT_STAGE), :]

    slot = {d: n for n, d in enumerate(d for d in DILATIONS if d != 1)}
    nwin_of = lambda d: ATT_CH // (d * WIN) + 1

    def grouped_rows(d, blk, res):
        if d == 1:
            return [(slice(a * ATT_SEG, (a + 1) * ATT_SEG),
                     pl.ds(pl.multiple_of(a * ATT_QUART + blk * ATT_SEG, ATT_SEG), ATT_SEG))
                    for a in range(ATT_STAGE)]
        if d == ATT_STAGE:
            return [(slice(None), pl.ds(pl.multiple_of(res * ATT_QUART + blk * ATT_QB, ATT_QB), ATT_QB))]
        hi = res // ATT_STAGE
        return [(slice(None), pl.ds((res - hi * ATT_STAGE) * ATT_QUART + hi, ATT_QB, stride=ATT_STAGE))]

    for d, sb in slot.items():
        nwin = nwin_of(d)

        def regroup_cur(t, carry, d=d, sb=sb, nwin=nwin):
            blk = t // d
            res = t - blk * d
            (_, idx), = grouped_rows(d, blk, res)
            qv, kv, vv = tmp[0, idx, :], tmp[1, idx, :], tmp[2, idx, :]
            q0 = pl.multiple_of(t * (2 * ATT_QB), 2 * ATT_QB)
            qd[sb, pl.ds(q0, ATT_QB), :] = jnp.where(in_a, qv, 0.0).astype(BF16)
            qd[sb, pl.ds(q0 + ATT_QB, ATT_QB), :] = jnp.where(in_a, 0.0, qv).astype(BF16)
            dst = pl.ds(pl.multiple_of((res * nwin + blk + 1) * ATT_QB, ATT_QB), ATT_QB)
            kd[sb, dst, :] = kv.astype(BF16)
            vd[sb, 0, dst, :] = jnp.where(in_a, vv, 1.0).astype(BF16)
            vd[sb, 1, dst, :] = jnp.where(in_a, 1.0, vv).astype(BF16)
            return carry

        def carry_prev(res, carry, sb=sb, nwin=nwin):
            dst = pl.ds(pl.multiple_of(res * nwin * ATT_QB, ATT_QB), ATT_QB)
            src = pl.ds(pl.multiple_of((res * nwin + nwin - 1) * ATT_QB, ATT_QB), ATT_QB)
            kd[sb, dst, :] = kd[sb, src, :]
            vd[sb, 0, dst, :] = vd[sb, 0, src, :]
            vd[sb, 1, dst, :] = vd[sb, 1, src, :]
            return carry

        def zero_prev(res, carry, sb=sb, nwin=nwin):
            dst = pl.ds(pl.multiple_of(res * nwin * ATT_QB, ATT_QB), ATT_QB)
            zeros = jnp.zeros((ATT_QB, LANES), BF16)
            kd[sb, dst, :] = zeros
            vd[sb, 0, dst, :] = zeros
            vd[sb, 1, dst, :] = zeros
            return carry

        @pl.when(first_chunk)
        def _(d=d, zero_prev=zero_prev):
            lax.fori_loop(0, d, zero_prev, 0)

        @pl.when(jnp.logical_not(first_chunk))
        def _(d=d, carry_prev=carry_prev):
            lax.fori_loop(0, d, carry_prev, 0)

        lax.fori_loop(0, ATT_TILES, regroup_cur, 0, unroll=2)

    for bi, d in enumerate(DILATIONS):
        def step_of(idx, d=d):
            if d == 1:
                return ATT_STAGE * (idx % ATT_SEG) + idx // ATT_SEG
            return idx

        dist = WIN + step_of(qi) - (step_of(kj % WIN) + WIN * (kj // WIN))
        valid = (dist >= 0) & (dist <= WIN)
        valid_first = valid & (kj >= WIN)
        distf = dist.astype(F32)
        for hh in range(2):
            alibi = (-slopes_ref[2 * hp + hh] * d) * distf
            rows = slice(hh * ATT_QB, (hh + 1) * ATT_QB)
            biasbuf[bi, 0, rows, :] = jnp.where(valid, alibi, NEG)
            biasbuf[bi, 1, rows, :] = jnp.where(valid_first, alibi, NEG)

        def operands(blk, res, t, d=d):
            if d != 1:
                sb, nwin = slot[d], nwin_of(d)
                q_rows = pl.ds(pl.multiple_of(t * (2 * ATT_QB), 2 * ATT_QB), 2 * ATT_QB)
                k_rows = pl.ds(pl.multiple_of((res * nwin + blk) * ATT_QB, ATT_QB), ATT_KB)
                return qd[sb, q_rows, :], kd[sb, k_rows, :], vd[sb, 0, k_rows, :], vd[sb, 1, k_rows, :]
            sb, nwin = slot[ATT_STAGE], nwin_of(ATT_STAGE)
            own = blk // ATT_STAGE
            part = blk - own * ATT_STAGE
            run = lambda start: pl.ds(pl.multiple_of(start, ATT_SEG), ATT_SEG)
            q_runs = [run(((own * ATT_STAGE + a) * 2 + hh) * ATT_QB + part * ATT_SEG)
                      for hh in range(2) for a in range(ATT_STAGE)]
            k_runs = [run((a * nwin + 1) * ATT_QB + (blk - 1 + w) * ATT_SEG)
                      for w in range(2) for a in range(ATT_STAGE)]
            gather = lambda ref_at, runs: jnp.concatenate([ref_at(r) for r in runs], axis=0)
            return (gather(lambda r: qd[sb, r, :], q_runs), gather(lambda r: kd[sb, r, :], k_runs),
                    gather(lambda r: vd[sb, 0, r, :], k_runs), gather(lambda r: vd[sb, 1, r, :], k_runs))

        def tile(t, carry, d=d, bi=bi, operands=operands):
            blk = t // d
            res = t - blk * d
            q2, k, va, vb = operands(blk, res, t)
            first = jnp.logical_and(first_chunk, blk == 0).astype(jnp.int32)
            s = lax.dot_general(q2, k, (((1,), (1,)), ((), ())), preferred_element_type=F32)
            s = s + biasbuf[bi, first]
            m = jnp.max(s, axis=-1, keepdims=True)
            p = jnp.exp(s - m).astype(BF16)
            oa = jnp.dot(p[:ATT_QB], va, preferred_element_type=F32)
            ob = jnp.dot(p[ATT_QB:], vb, preferred_element_type=F32)
            ma = jnp.broadcast_to(m[:ATT_QB], (ATT_QB, LANES))
            mb = jnp.broadcast_to(m[ATT_QB:], (ATT_QB, LANES))
            num = jnp.where(in_a, oa, ob)
            den = jnp.where(in_a, ob, oa)
            m_num = jnp.where(in_a, ma, mb)
            m_den = jnp.where(in_a, mb, ma)
            for rows, dst in grouped_rows(d, blk, res):
                xbuf[bi, dst, :] = num[rows]
                ybuf[bi, dst, :] = den[rows]
                mbuf[bi, 0, dst, :] = m_num[rows]
                mbuf[bi, 1, dst, :] = m_den[rows]
            return carry

        lax.fori_loop(0, ATT_TILES, tile, 0, unroll=ATT_UNROLL)

    def combine(t, carry):
        rows = pl.ds(pl.multiple_of(t * ATT_QB, ATT_QB), ATT_QB)

        def weighted_sum(which, vals):
            ms = [mbuf[bi, which, rows, :] for bi in range(len(DILATIONS))]
            mm = functools.reduce(jnp.maximum, ms)
            return sum(jnp.exp(mb - mm) * vals[bi, rows, :] for bi, mb in enumerate(ms))

        num = weighted_sum(0, xbuf)
        den = weighted_sum(1, ybuf)
        a = t // (ATT_QUART // ATT_QB)
        blk = t - a * (ATT_QUART // ATT_QB)
        nat[pl.ds(blk * (ATT_STAGE * ATT_QB) + a, ATT_QB, stride=ATT_STAGE), :] = (
            num / pltpu.roll(den, HEAD_DIM, axis=1))
        return carry

    lax.fori_loop(0, ATT_TILES, combine, 0, unroll=2)
    o_ref[0] = nat[...].astype(o_ref.dtype)


def _attn(q, k, v, slopes):
    bsz, s, _ = q.shape
    blk = (1, ATT_CH, LANES)
    nb = len(DILATIONS)
    cur = pl.BlockSpec(blk, lambda b, h, c, sl: (b, c, h))
    grid_spec = pltpu.PrefetchScalarGridSpec(
        num_scalar_prefetch=1,
        grid=(bsz, ATT_WIDTH // LANES, s // ATT_CH),
        in_specs=[cur, cur, cur],
        out_specs=cur,
        scratch_shapes=[
            pltpu.VMEM((3, ATT_CH, LANES), F32),
            pltpu.VMEM((nb - 1, 2 * ATT_CH, LANES), BF16),
            pltpu.VMEM((nb - 1, 2 * ATT_CH, LANES), BF16),
            pltpu.VMEM((nb - 1, 2, 2 * ATT_CH, LANES), BF16),
            pltpu.VMEM((nb, 2, 2 * ATT_QB, ATT_KB), F32),
            pltpu.VMEM((nb, ATT_CH, LANES), F32),
            pltpu.VMEM((nb, ATT_CH, LANES), F32),
            pltpu.VMEM((nb, 2, ATT_CH, LANES), F32),
            pltpu.VMEM((ATT_CH, LANES), F32),
        ],
    )
    return pl.pallas_call(
        _attn_kernel,
        grid_spec=grid_spec,
        out_shape=jax.ShapeDtypeStruct((bsz, s, ATT_WIDTH), BF16),
        compiler_params=pltpu.CompilerParams(
            dimension_semantics=("parallel", "arbitrary", "arbitrary"),
            vmem_limit_bytes=VMEM_LIMIT),
        name="attn",
    )(slopes, q, k, v)


OUT_TM = 512


def _outproj_kernel(x_ref, yc_ref, ya_ref, wc_ref, wa_ref, o_ref):
    o_ref[...] = (x_ref[...]
                  + jnp.dot(yc_ref[...], wc_ref[...], preferred_element_type=F32)
                  + jnp.dot(ya_ref[...], wa_ref[...], preferred_element_type=F32))


def _outproj(x, yc, ya, wc, wa):
    n = x.shape[0]
    return pl.pallas_call(
        _outproj_kernel,
        grid=(n // OUT_TM,),
        in_specs=[
            pl.BlockSpec((OUT_TM, D_MODEL), lambda i: (i, 0)),
            pl.BlockSpec((OUT_TM, CONV_CH), lambda i: (i, 0)),
            pl.BlockSpec((OUT_TM, ATT_WIDTH), lambda i: (i, 0)),
            pl.BlockSpec((CONV_CH, D_MODEL), lambda i: (0, 0)),
            pl.BlockSpec((ATT_WIDTH, D_MODEL), lambda i: (0, 0)),
        ],
        out_specs=pl.BlockSpec((OUT_TM, D_MODEL), lambda i: (i, 0)),
        out_shape=jax.ShapeDtypeStruct((n, D_MODEL), F32),
        compiler_params=pltpu.CompilerParams(
            dimension_semantics=("parallel",),
            vmem_limit_bytes=VMEM_LIMIT),
        name="out_proj",
    )(x, yc, ya, wc, wa)


def _layer(x2d, bsz, s, p):
    n = bsz * s
    x2d = _ffn(x2d, p["ffn1_norm_g"], p["ffn1_w_gate"], p["ffn1_w_up"], p["ffn1_w_down"])
    yc, q, k, v = _mixin(x2d, s, p["mix_norm_g"], p["w_in"], p["q_norm_g"], p["k_norm_g"], p["head_ones"],
                         p["conv_w_dw"], p["conv_b_dw"], p["conv_ln_g"], p["conv_ln_b"])
    ya =_attn(q.reshape(bsz, s, ATT_WIDTH), k.reshape(bsz, s, ATT_WIDTH), v.reshape(bsz, s, ATT_WIDTH),
               p["slopes"])
    x2d = _outproj(x2d, yc.reshape(n, CONV_CH), ya.reshape(n, ATT_WIDTH), p["w_out_conv"], p["w_out_att"])
    return _ffn(x2d, p["ffn2_norm_g"], p["ffn2_w_gate"], p["ffn2_w_up"], p["ffn2_w_down"])


def kernel(x, ffn1_norm_g, ffn1_w_gate, ffn1_w_up, ffn1_w_down, mix_norm_g, w_in, conv_w_dw, conv_b_dw,
           conv_ln_g, conv_ln_b, q_norm_g, k_norm_g, w_out, ffn2_norm_g, ffn2_w_gate, ffn2_w_up, ffn2_w_down):
    bsz, s, _ = x.shape
    depth = w_in.shape[0]
    row = lambda a: a.reshape(1, -1).astype(F32)
    head_ids = jnp.arange(HEAD_GROUP) // HEAD_DIM
    head_ones = (head_ids[:, None] == head_ids[None, :]).astype(BF16)
    slopes = 2.0 ** (-ALIBI_MAX_BIAS * jnp.arange(1, N_HEADS + 1, dtype=F32) / N_HEADS)
    x2d = x.reshape(bsz * s, D_MODEL)
    for l in range(depth):
        p = dict(
            ffn1_norm_g=row(ffn1_norm_g[l]), ffn1_w_gate=ffn1_w_gate[l].astype(BF16),
            ffn1_w_up=ffn1_w_up[l].astype(BF16), ffn1_w_down=ffn1_w_down[l].astype(BF16),
            mix_norm_g=row(mix_norm_g[l]), w_in=w_in[l].astype(BF16),
            q_norm_g=row(jnp.tile(q_norm_g[l], N_HEADS)), k_norm_g=row(jnp.tile(k_norm_g[l], N_HEADS)),
            head_ones=head_ones, slopes=slopes,
            conv_w_dw=jnp.broadcast_to(conv_w_dw[l].astype(F32)[:, None, :], (CONV_WIDTH, SUBLANES, CONV_CH)),
            conv_b_dw=row(conv_b_dw[l]),
            conv_ln_g=row(conv_ln_g[l]), conv_ln_b=row(conv_ln_b[l]),
            w_out_conv=w_out[l, :CONV_CH].astype(BF16), w_out_att=w_out[l, CONV_CH:].astype(BF16),
            ffn2_norm_g=row(ffn2_norm_g[l]), ffn2_w_gate=ffn2_w_gate[l].astype(BF16),
            ffn2_w_up=ffn2_w_up[l].astype(BF16), ffn2_w_down=ffn2_w_down[l].astype(BF16),
        )
        x2d = _layer(x2d, bsz, s, p)
    return x2d.reshape(bsz, s, D_MODEL)
```

```python
import functools

import jax
import jax.numpy as jnp
from jax import lax
from jax.experimental import pallas as pl
from jax.experimental.pallas import tpu as pltpu

D_MODEL = 2048
D_FF = 5632
CONV_CH = 1024
CONV_WIDTH = 31
N_HEADS = 16
HEAD_DIM = 64
ATT_WIDTH = N_HEADS * HEAD_DIM
DILATIONS = (1, 4, 16)
WIN = 128
ALIBI_MAX_BIAS = 8.0
EPS = 1e-6

LANES = 128
SUBLANES = 8
VMEM_LIMIT = 56 * 1024 * 1024

F32 = jnp.float32
BF16 = jnp.bfloat16
NEG = -1e30


FFN_TM = 1024
FFN_TF = 512


def _ffn_kernel(x_ref, g_ref, wg_ref, wu_ref, wd_ref, o_ref, h_ref, r_ref):
    j = pl.program_id(1)

    def chunk(h, r):
        r = jnp.concatenate([r] * (FFN_TF // LANES), axis=1)
        gate = jnp.dot(h, wg_ref[...], preferred_element_type=F32) * r
        up = jnp.dot(h, wu_ref[...], preferred_element_type=F32) * r
        a = (gate * jax.nn.sigmoid(gate) * (0.5 * up)).astype(BF16)
        return jnp.dot(a, wd_ref[...], preferred_element_type=F32)

    @pl.when(j == 0)
    def _():
        x = x_ref[...]
        h = (x * g_ref[...]).astype(BF16)
        r = jnp.broadcast_to(lax.rsqrt(jnp.mean(x * x, axis=-1, keepdims=True) + EPS), (FFN_TM, LANES))
        h_ref[...] = h
        r_ref[...] = r
        o_ref[...] = x + chunk(h, r)

    @pl.when(j > 0)
    def _():
        o_ref[...] += chunk(h_ref[...], r_ref[...])


def _ffn(x, g, wg, wu, wd):
    n = x.shape[0]
    return pl.pallas_call(
        _ffn_kernel,
        grid=(n // FFN_TM, D_FF // FFN_TF),
        in_specs=[
            pl.BlockSpec((FFN_TM, D_MODEL), lambda i, j: (i, 0)),
            pl.BlockSpec((1, D_MODEL), lambda i, j: (0, 0)),
            pl.BlockSpec((D_MODEL, FFN_TF), lambda i, j: (0, j)),
            pl.BlockSpec((D_MODEL, FFN_TF), lambda i, j: (0, j)),
            pl.BlockSpec((FFN_TF, D_MODEL), lambda i, j: (j, 0)),
        ],
        out_specs=pl.BlockSpec((FFN_TM, D_MODEL), lambda i, j: (i, 0)),
        out_shape=jax.ShapeDtypeStruct((n, D_MODEL), F32),
        scratch_shapes=[pltpu.VMEM((FFN_TM, D_MODEL), BF16), pltpu.VMEM((FFN_TM, LANES), F32)],
        compiler_params=pltpu.CompilerParams(
            dimension_semantics=("parallel", "arbitrary"),
            vmem_limit_bytes=VMEM_LIMIT),
        name="ffn",
    )(x, g, wg, wu, wd)


INP_TM = 256
HEAD_GROUP = 256


def _head_norm_group(z, gain, e):
    ss = z * z
    hi = ss.astype(BF16)
    lo = (ss - hi.astype(F32)).astype(BF16)
    tot = jnp.dot(hi, e, preferred_element_type=F32) + jnp.dot(lo, e, preferred_element_type=F32)
    return z * lax.rsqrt(tot * (1.0 / HEAD_DIM) + EPS) * gain


CONV_TS = INP_TM
CONV_HALO = 32
CONV_RC = 64
CONV_PARTS = 4
CONV_LN_ROWS = 32
CONV_PAD = CONV_HALO - (CONV_WIDTH - 1)
CONV_SH_ROWS = CONV_TS + CONV_HALO - SUBLANES


def _mixin_kernel(tiles_per_seq, x_ref, g_ref, w_ref, qg_ref, kg_ref, e_ref, cw_ref, cb_ref, cg_ref,
                  cbeta_ref, yc_ref, q_ref, k_ref, v_ref, buf_ref, sh_ref, pre_ref):
    i = pl.program_id(0)
    halo = pl.ds(0, CONV_HALO)
    tail = pl.ds(CONV_TS, CONV_HALO)

    @pl.when(i % tiles_per_seq == 0)
    def _():
        buf_ref[halo, :] = jnp.zeros((CONV_HALO, CONV_CH), F32)

    @pl.when(i % tiles_per_seq != 0)
    def _():
        buf_ref[halo, :] = buf_ref[tail, :]

    x = x_ref[...]
    r = lax.rsqrt(jnp.mean(x * x, axis=-1, keepdims=True) + EPS)
    xg = (x * g_ref[...]).astype(BF16)

    def proj(col0, width):
        return jnp.dot(xg, w_ref[:, col0:col0 + width], preferred_element_type=F32) * r

    bias = cb_ref[...]
    gain = cg_ref[...]
    beta = cbeta_ref[...]
    part = CONV_CH // CONV_PARTS
    groups = CONV_RC // SUBLANES

    def head_norm(z, gain_ref):
        groups = [slice(c * HEAD_GROUP, (c + 1) * HEAD_GROUP) for c in range(ATT_WIDTH // HEAD_GROUP)]
        return jnp.concatenate([_head_norm_group(z[:, s], gain_ref[:, s], e_ref[...]) for s in groups], axis=1)

    def project_q():
        q_ref[...] = head_norm(proj(2 * CONV_CH, ATT_WIDTH), qg_ref) * (HEAD_DIM ** -0.5)
        return q_ref

    def project_k():
        k_ref[...] = head_norm(proj(2 * CONV_CH + ATT_WIDTH, ATT_WIDTH), kg_ref)
        return k_ref

    def project_v():
        v_ref[...] = proj(2 * CONV_CH + 2 * ATT_WIDTH, ATT_WIDTH)
        return v_ref

    never = i < 0
    att_projections = [project_q, project_k, project_v]

    for n, lanes in enumerate(slice(n * part, (n + 1) * part) for n in range(CONV_PARTS)):
        glu = proj(lanes.start, part) * jax.nn.sigmoid(proj(CONV_CH + lanes.start, part))
        if n > 0 and att_projections:
            stored = att_projections.pop(0)()
            glu = glu + jnp.where(never, stored[0:1, 0:part], 0.0)
        buf_ref[CONV_HALO:, lanes] = glu
        for s in range(1, SUBLANES):
            sh_ref[s - 1, :, lanes] = buf_ref[s:s + CONV_SH_ROWS, lanes]
        for c in range(CONV_TS // CONV_RC):
            r0 = c * CONV_RC
            acc = jnp.broadcast_to(bias[:, lanes], (groups, SUBLANES, part))
            for t in range(CONV_WIDTH):
                off = CONV_PAD + t
                phase, base = off % SUBLANES, off - off % SUBLANES
                if phase == 0:
                    rows = buf_ref[r0 + base:r0 + base + CONV_RC, lanes]
                else:
                    rows = sh_ref[phase - 1, r0 + base:r0 + base + CONV_RC, lanes]
                acc = acc + rows.reshape(groups, SUBLANES, part) * cw_ref[t, :, lanes][None]
            pre_ref[r0:r0 + CONV_RC, lanes] = acc.reshape(CONV_RC, part)
    for project in att_projections:
        project()

    for r0 in range(0, CONV_TS, CONV_LN_ROWS):
        acc = pre_ref[r0:r0 + CONV_LN_ROWS, :]
        mu = jnp.mean(acc, axis=-1, keepdims=True)
        cen = acc - mu
        var = jnp.mean(cen * cen, axis=-1, keepdims=True)
        y = cen * lax.rsqrt(var + EPS) * gain + beta
        yc_ref[r0:r0 + CONV_LN_ROWS, :] = (y * jax.nn.sigmoid(y)).astype(yc_ref.dtype)


def _mixin(x, seq_len, g, w, qg, kg, e, cw, cb, cg, cbeta):
    n = x.shape[0]
    d_in = w.shape[1]
    att = jax.ShapeDtypeStruct((n, ATT_WIDTH), F32)
    tile = lambda width: pl.BlockSpec((INP_TM, width), lambda i: (i, 0))
    const = lambda shape, **kw: pl.BlockSpec(shape, lambda i: (0,) * len(shape), **kw)
    return pl.pallas_call(
        functools.partial(_mixin_kernel, seq_len // INP_TM),
        grid=(n // INP_TM,),
        in_specs=[
            tile(D_MODEL),
            const((1, D_MODEL)),
            const((D_MODEL, d_in), pipeline_mode=pl.Buffered(1)),
            const((1, ATT_WIDTH)),
            const((1, ATT_WIDTH)),
            const((HEAD_GROUP, HEAD_GROUP)),
            const((CONV_WIDTH, SUBLANES, CONV_CH)),
            const((1, CONV_CH)), const((1, CONV_CH)), const((1, CONV_CH)),
        ],
        out_specs=[tile(CONV_CH), tile(ATT_WIDTH), tile(ATT_WIDTH), tile(ATT_WIDTH)],
        out_shape=[jax.ShapeDtypeStruct((n, CONV_CH), BF16), att, att, att],
        scratch_shapes=[
            pltpu.VMEM((CONV_TS + CONV_HALO, CONV_CH), F32),
            pltpu.VMEM((SUBLANES - 1, CONV_SH_ROWS, CONV_CH), F32),
            pltpu.VMEM((CONV_TS, CONV_CH), F32),
        ],
        compiler_params=pltpu.CompilerParams(
            dimension_semantics=("arbitrary",),
            vmem_limit_bytes=VMEM_LIMIT),
        name="mix_in",
    )(x, g, w, qg, kg, e, cw, cb, cg, cbeta)


ATT_CH = 2048
ATT_QB = WIN
ATT_KB = 2 * WIN
ATT_TILES = ATT_CH // ATT_QB
ATT_STAGE = 4
ATT_QUART = ATT_CH // ATT_STAGE
ATT_SEG = ATT_QB // ATT_STAGE
ATT_UNROLL = 16


def _attn_kernel(slopes_ref, q_ref, kc_ref, vc_ref, o_ref,
                 tmp, qd, kd, vd, biasbuf, xbuf, ybuf, mbuf, nat):
    hp = pl.program_id(1)
    c = pl.program_id(2)
    first_chunk = c == 0

    lane = lax.broadcasted_iota(jnp.int32, (1, LANES), 1)
    in_a = lane < HEAD_DIM
    qi = lax.broadcasted_iota(jnp.int32, (ATT_QB, ATT_KB), 0)
    kj = lax.broadcasted_iota(jnp.int32, (ATT_QB, ATT_KB), 1)

    for ai, ref in enumerate((q_ref, kc_ref, vc_ref)):
        for a in range(ATT_STAGE):
            tmp[ai, a * ATT_QUART:(a + 1) * ATT_QUART, :] = ref[0, pl.ds(a, ATT_QUART, stride=ATT_STAGE), :]

    slot = {d: n for n, d in enumerate(d for d in DILATIONS if d != 1)}
    nwin_of = lambda d: ATT_CH // (d * WIN) + 1

    def grouped_rows(d, blk, res):
        if d == 1:
            return [(slice(a * ATT_SEG, (a + 1) * ATT_SEG),
                     pl.ds(pl.multiple_of(a * ATT_QUART + blk * ATT_SEG, ATT_SEG), ATT_SEG))
                    for a in range(ATT_STAGE)]
        if d == ATT_STAGE:
            return [(slice(None), pl.ds(pl.multiple_of(res * ATT_QUART + blk * ATT_QB, ATT_QB), ATT_QB))]
        hi = res // ATT_STAGE
        return [(slice(None), pl.ds((res - hi * ATT_STAGE) * ATT_QUART + hi, ATT_QB, stride=ATT_STAGE))]

    for d, sb in slot.items():
        nwin = nwin_of(d)

        def regroup_cur(t, carry, d=d, sb=sb, nwin=nwin):
            blk = t // d
            res = t - blk * d
            (_, idx), = grouped_rows(d, blk, res)
            qv, kv, vv = tmp[0, idx, :], tmp[1, idx, :], tmp[2, idx, :]
            q0 = pl.multiple_of(t * (2 * ATT_QB), 2 * ATT_QB)
            qd[sb, pl.ds(q0, ATT_QB), :] = jnp.where(in_a, qv, 0.0).astype(BF16)
            qd[sb, pl.ds(q0 + ATT_QB, ATT_QB), :] = jnp.where(in_a, 0.0, qv).astype(BF16)
            dst = pl.ds(pl.multiple_of((res * nwin + blk + 1) * ATT_QB, ATT_QB), ATT_QB)
            kd[sb, dst, :] = kv.astype(BF16)
            vd[sb, 0, dst, :] = jnp.where(in_a, vv, 1.0).astype(BF16)
            vd[sb, 1, dst, :] = jnp.where(in_a, 1.0, vv).astype(BF16)
            return carry

        def carry_prev(res, carry, sb=sb, nwin=nwin):
            dst = pl.ds(pl.multiple_of(res * nwin * ATT_QB, ATT_QB), ATT_QB)
            src = pl.ds(pl.multiple_of((res * nwin + nwin - 1) * ATT_QB, ATT_QB), ATT_QB)
            kd[sb, dst, :] = kd[sb, src, :]
            vd[sb, 0, dst, :] = vd[sb, 0, src, :]
            vd[sb, 1, dst, :] = vd[sb, 1, src, :]
            return carry

        def zero_prev(res, carry, sb=sb, nwin=nwin):
            dst = pl.ds(pl.multiple_of(res * nwin * ATT_QB, ATT_QB), ATT_QB)
            zeros = jnp.zeros((ATT_QB, LANES), BF16)
            kd[sb, dst, :] = zeros
            vd[sb, 0, dst, :] = zeros
            vd[sb, 1, dst, :] = zeros
            return carry

        @pl.when(first_chunk)
        def _(d=d, zero_prev=zero_prev):
            lax.fori_loop(0, d, zero_prev, 0)

        @pl.when(jnp.logical_not(first_chunk))
        def _(d=d, carry_prev=carry_prev):
            lax.fori_loop(0, d, carry_prev, 0)

        lax.fori_loop(0, ATT_TILES, regroup_cur, 0, unroll=2)

    for bi, d in enumerate(DILATIONS):
        def step_of(idx, d=d):
            if d == 1:
                return ATT_STAGE * (idx % ATT_SEG) + idx // ATT_SEG
            return idx

        dist = WIN + step_of(qi) - (step_of(kj % WIN) + WIN * (kj // WIN))
        valid = (dist >= 0) & (dist <= WIN)
        valid_first = valid & (kj >= WIN)
        distf = dist.astype(F32)
        for hh in range(2):
            alibi = (-slopes_ref[2 * hp + hh] * d) * distf
            rows = slice(hh * ATT_QB, (hh + 1) * ATT_QB)
            biasbuf[bi, 0, rows, :] = jnp.where(valid, alibi, NEG)
            biasbuf[bi, 1, rows, :] = jnp.where(valid_first, alibi, NEG)

        def operands(blk, res, t, d=d):
            if d != 1:
                sb, nwin = slot[d], nwin_of(d)
                q_rows = pl.ds(pl.multiple_of(t * (2 * ATT_QB), 2 * ATT_QB), 2 * ATT_QB)
                k_rows = pl.ds(pl.multiple_of((res * nwin + blk) * ATT_QB, ATT_QB), ATT_KB)
                return qd[sb, q_rows, :], kd[sb, k_rows, :], vd[sb, 0, k_rows, :], vd[sb, 1, k_rows, :]
            sb, nwin = slot[ATT_STAGE], nwin_of(ATT_STAGE)
            own = blk // ATT_STAGE
            part = blk - own * ATT_STAGE
            run = lambda start: pl.ds(pl.multiple_of(start, ATT_SEG), ATT_SEG)
            q_runs = [run(((own * ATT_STAGE + a) * 2 + hh) * ATT_QB + part * ATT_SEG)
                      for hh in range(2) for a in range(ATT_STAGE)]
            k_runs = [run((a * nwin + 1) * ATT_QB + (blk - 1 + w) * ATT_SEG)
                      for w in range(2) for a in range(ATT_STAGE)]
            gather = lambda ref_at, runs: jnp.concatenate([ref_at(r) for r in runs], axis=0)
            return (gather(lambda r: qd[sb, r, :], q_runs), gather(lambda r: kd[sb, r, :], k_runs),
                    gather(lambda r: vd[sb, 0, r, :], k_runs), gather(lambda r: vd[sb, 1, r, :], k_runs))

        def tile(t, carry, d=d, bi=bi, operands=operands):
            blk = t // d
            res = t - blk * d
            q2, k, va, vb = operands(blk, res, t)
            first = jnp.logical_and(first_chunk, blk == 0).astype(jnp.int32)
            s = lax.dot_general(q2, k, (((1,), (1,)), ((), ())), preferred_element_type=F32)
            s = s + biasbuf[bi, first]
            m = jnp.max(s, axis=-1, keepdims=True)
            p = jnp.exp(s - m).astype(BF16)
            oa = jnp.dot(p[:ATT_QB], va, preferred_element_type=F32)
            ob = jnp.dot(p[ATT_QB:], vb, preferred_element_type=F32)
            ma = jnp.broadcast_to(m[:ATT_QB], (ATT_QB, LANES))
            mb = jnp.broadcast_to(m[ATT_QB:], (ATT_QB, LANES))
            num = jnp.where(in_a, oa, ob)
            den = jnp.where(in_a, ob, oa)
            m_num = jnp.where(in_a, ma, mb)
            m_den = jnp.where(in_a, mb, ma)
            for rows, dst in grouped_rows(d, blk, res):
                xbuf[bi, dst, :] = num[rows]
                ybuf[bi, dst, :] = den[rows]
                mbuf[bi, 0, dst, :] = m_num[rows]
                mbuf[bi, 1, dst, :] = m_den[rows]
            return carry

        lax.fori_loop(0, ATT_TILES, tile, 0, unroll=ATT_UNROLL)

    def combine(t, carry):
        rows = pl.ds(pl.multiple_of(t * ATT_QB, ATT_QB), ATT_QB)

        def weighted_sum(which, vals):
            ms = [mbuf[bi, which, rows, :] for bi in range(len(DILATIONS))]
            mm = functools.reduce(jnp.maximum, ms)
            return sum(jnp.exp(mb - mm) * vals[bi, rows, :] for bi, mb in enumerate(ms))

        num = weighted_sum(0, xbuf)
        den = weighted_sum(1, ybuf)
        a = t // (ATT_QUART // ATT_QB)
        blk = t - a * (ATT_QUART // ATT_QB)
        nat[pl.ds(blk * (ATT_STAGE * ATT_QB) + a, ATT_QB, stride=ATT_STAGE), :] = (
            num / pltpu.roll(den, HEAD_DIM, axis=1))
        return carry

    lax.fori_loop(0, ATT_TILES, combine, 0, unroll=2)
    o_ref[0] = nat[...].astype(o_ref.dtype)


def _attn(q, k, v, slopes):
    bsz, s, _ = q.shape
    blk = (1, ATT_CH, LANES)
    nb = len(DILATIONS)
    cur = pl.BlockSpec(blk, lambda b, h, c, sl: (b, c, h))
    grid_spec = pltpu.PrefetchScalarGridSpec(
        num_scalar_prefetch=1,
        grid=(bsz, ATT_WIDTH // LANES, s // ATT_CH),
        in_specs=[cur, cur, cur],
        out_specs=cur,
        scratch_shapes=[
            pltpu.VMEM((3, ATT_CH, LANES), F32),
            pltpu.VMEM((nb - 1, 2 * ATT_CH, LANES), BF16),
            pltpu.VMEM((nb - 1, 2 * ATT_CH, LANES), BF16),
            pltpu.VMEM((nb - 1, 2, 2 * ATT_CH, LANES), BF16),
            pltpu.VMEM((nb, 2, 2 * ATT_QB, ATT_KB), F32),
            pltpu.VMEM((nb, ATT_CH, LANES), F32),
            pltpu.VMEM((nb, ATT_CH, LANES), F32),
            pltpu.VMEM((nb, 2, ATT_CH, LANES), F32),
            pltpu.VMEM((ATT_CH, LANES), F32),
        ],
    )
    return pl.pallas_call(
        _attn_kernel,
        grid_spec=grid_spec,
        out_shape=jax.ShapeDtypeStruct((bsz, s, ATT_WIDTH), BF16),
        compiler_params=pltpu.CompilerParams(
            dimension_semantics=("parallel", "arbitrary", "arbitrary"),
            vmem_limit_bytes=VMEM_LIMIT),
        name="attn",
    )(slopes, q, k, v)


OUT_TM = 512


def _outproj_kernel(x_ref, yc_ref, ya_ref, wc_ref, wa_ref, o_ref):
    o_ref[...] = (x_ref[...]
                  + jnp.dot(yc_ref[...], wc_ref[...], preferred_element_type=F32)
                  + jnp.dot(ya_ref[...], wa_ref[...], preferred_element_type=F32))


def _outproj(x, yc, ya, wc, wa):
    n = x.shape[0]
    return pl.pallas_call(
        _outproj_kernel,
        grid=(n // OUT_TM,),
        in_specs=[
            pl.BlockSpec((OUT_TM, D_MODEL), lambda i: (i, 0)),
            pl.BlockSpec((OUT_TM, CONV_CH), lambda i: (i, 0)),
            pl.BlockSpec((OUT_TM, ATT_WIDTH), lambda i: (i, 0)),
            pl.BlockSpec((CONV_CH, D_MODEL), lambda i: (0, 0)),
            pl.BlockSpec((ATT_WIDTH, D_MODEL), lambda i: (0, 0)),
        ],
        out_specs=pl.BlockSpec((OUT_TM, D_MODEL), lambda i: (i, 0)),
        out_shape=jax.ShapeDtypeStruct((n, D_MODEL), F32),
        compiler_params=pltpu.CompilerParams(
            dimension_semantics=("parallel",),
            vmem_limit_bytes=VMEM_LIMIT),
        name="out_proj",
    )(x, yc, ya, wc, wa)


def _layer(x2d, bsz, s, p):
    n = bsz * s
    x2d = _ffn(x2d, p["ffn1_norm_g"], p["ffn1_w_gate"], p["ffn1_w_up"], p["ffn1_w_down"])
    yc, q, k, v = _mixin(x2d, s, p["mix_norm_g"], p["w_in"], p["q_norm_g"], p["k_norm_g"], p["head_ones"],
                         p["conv_w_dw"], p["conv_b_dw"], p["conv_ln_g"], p["conv_ln_b"])
    ya =_attn(q.reshape(bsz, s, ATT_WIDTH), k.reshape(bsz, s, ATT_WIDTH), v.reshape(bsz, s, ATT_WIDTH),
               p["slopes"])
    x2d = _outproj(x2d, yc.reshape(n, CONV_CH), ya.reshape(n, ATT_WIDTH), p["w_out_conv"], p["w_out_att"])
    return _ffn(x2d, p["ffn2_norm_g"], p["ffn2_w_gate"], p["ffn2_w_up"], p["ffn2_w_down"])


def kernel(x, ffn1_norm_g, ffn1_w_gate, ffn1_w_up, ffn1_w_down, mix_norm_g, w_in, conv_w_dw, conv_b_dw,
           conv_ln_g, conv_ln_b, q_norm_g, k_norm_g, w_out, ffn2_norm_g, ffn2_w_gate, ffn2_w_up, ffn2_w_down):
    bsz, s, _ = x.shape
    depth = w_in.shape[0]
    row = lambda a: a.reshape(1, -1).astype(F32)
    head_ids = jnp.arange(HEAD_GROUP) // HEAD_DIM
    head_ones = (head_ids[:, None] == head_ids[None, :]).astype(BF16)
    slopes = 2.0 ** (-ALIBI_MAX_BIAS * jnp.arange(1, N_HEADS + 1, dtype=F32) / N_HEADS)
    x2d = x.reshape(bsz * s, D_MODEL)
    for l in range(depth):
        p = dict(
            ffn1_norm_g=row(ffn1_norm_g[l]), ffn1_w_gate=ffn1_w_gate[l].astype(BF16),
            ffn1_w_up=ffn1_w_up[l].astype(BF16), ffn1_w_down=ffn1_w_down[l].astype(BF16),
            mix_norm_g=row(mix_norm_g[l]), w_in=w_in[l].astype(BF16),
            q_norm_g=row(jnp.tile(q_norm_g[l], N_HEADS)), k_norm_g=row(jnp.tile(k_norm_g[l], N_HEADS)),
            head_ones=head_ones, slopes=slopes,
            conv_w_dw=jnp.broadcast_to(conv_w_dw[l].astype(F32)[:, None, :], (CONV_WIDTH, SUBLANES, CONV_CH)),
            conv_b_dw=row(conv_b_dw[l]),
            conv_ln_g=row(conv_ln_g[l]), conv_ln_b=row(conv_ln_b[l]),
            w_out_conv=w_out[l, :CONV_CH].astype(BF16), w_out_att=w_out[l, CONV_CH:].astype(BF16),
            ffn2_norm_g=row(ffn2_norm_g[l]), ffn2_w_gate=ffn2_w_gate[l].astype(BF16),
            ffn2_w_up=ffn2_w_up[l].astype(BF16), ffn2_w_down=ffn2_w_down[l].astype(BF16),
        )
        x2d = _layer(x2d, bsz, s, p)
    return x2d.reshape(bsz, s, D_MODEL)
```

```python
import functools

import jax
import jax.numpy as jnp
from jax import lax
from jax.experimental import pallas as pl
from jax.experimental.pallas import tpu as pltpu

D_MODEL = 2048
D_FF = 5632
CONV_CH = 1024
CONV_WIDTH = 31
N_HEADS = 16
HEAD_DIM = 64
ATT_WIDTH = N_HEADS * HEAD_DIM
DILATIONS = (1, 4, 16)
WIN = 128
ALIBI_MAX_BIAS = 8.0
EPS = 1e-6

LANES = 128
SUBLANES = 8
BF16_SUBLANES = 16
VMEM_LIMIT = 56 * 1024 * 1024

F32 = jnp.float32
BF16 = jnp.bfloat16
NEG = -1e30


FFN_TM = 1024
FFN_TF = 512


def _ffn_kernel(x_ref, g_ref, wg_ref, wu_ref, wd_ref, o_ref, h_ref, r_ref):
    j = pl.program_id(1)

    def chunk(h, r):
        r = jnp.concatenate([r] * (FFN_TF // LANES), axis=1)
        gate = jnp.dot(h, wg_ref[...], preferred_element_type=F32) * r
        up = jnp.dot(h, wu_ref[...], preferred_element_type=F32) * r
        a = (gate * jax.nn.sigmoid(gate) * (0.5 * up)).astype(BF16)
        return jnp.dot(a, wd_ref[...], preferred_element_type=F32)

    @pl.when(j == 0)
    def _():
        x = x_ref[...]
        h = (x * g_ref[...]).astype(BF16)
        r = jnp.broadcast_to(lax.rsqrt(jnp.mean(x * x, axis=-1, keepdims=True) + EPS), (FFN_TM, LANES))
        h_ref[...] = h
        r_ref[...] = r
        o_ref[...] = x + chunk(h, r)

    @pl.when(j > 0)
    def _():
        o_ref[...] += chunk(h_ref[...], r_ref[...])


def _ffn(x, g, wg, wu, wd):
    n = x.shape[0]
    return pl.pallas_call(
        _ffn_kernel,
        grid=(n // FFN_TM, D_FF // FFN_TF),
        in_specs=[
            pl.BlockSpec((FFN_TM, D_MODEL), lambda i, j: (i, 0)),
            pl.BlockSpec((1, D_MODEL), lambda i, j: (0, 0)),
            pl.BlockSpec((D_MODEL, FFN_TF), lambda i, j: (0, j)),
            pl.BlockSpec((D_MODEL, FFN_TF), lambda i, j: (0, j)),
            pl.BlockSpec((FFN_TF, D_MODEL), lambda i, j: (j, 0)),
        ],
        out_specs=pl.BlockSpec((FFN_TM, D_MODEL), lambda i, j: (i, 0)),
        out_shape=jax.ShapeDtypeStruct((n, D_MODEL), F32),
        scratch_shapes=[pltpu.VMEM((FFN_TM, D_MODEL), BF16), pltpu.VMEM((FFN_TM, LANES), F32)],
        compiler_params=pltpu.CompilerParams(
            dimension_semantics=("parallel", "arbitrary"),
            vmem_limit_bytes=VMEM_LIMIT),
        name="ffn",
    )(x, g, wg, wu, wd)


INP_TM = 256
HEAD_GROUP = 256


def _head_norm_group(z, gain, e):
    ss = z * z
    hi = ss.astype(BF16)
    lo = (ss - hi.astype(F32)).astype(BF16)
    tot = jnp.dot(hi, e, preferred_element_type=F32) + jnp.dot(lo, e, preferred_element_type=F32)
    return z * lax.rsqrt(tot * (1.0 / HEAD_DIM) + EPS) * gain


CONV_TS = INP_TM
CONV_HALO = 32
CONV_RC = 64
CONV_PARTS = 4
CONV_LN_ROWS = 32
CONV_PAD = CONV_HALO - (CONV_WIDTH - 1)
CONV_SH_ROWS = CONV_TS + CONV_HALO - SUBLANES


def _mixin_kernel(tiles_per_seq, x_ref, g_ref, w_ref, qg_ref, kg_ref, e_ref, cw_ref, cb_ref, cg_ref,
                  cbeta_ref, wo_ref, yc_ref, q_ref, k_ref, v_ref, wo_out, buf_ref, sh_ref, pre_ref):
    i = pl.program_id(0)
    wo_out[...] = wo_ref[...].astype(BF16)
    halo = pl.ds(0, CONV_HALO)
    tail = pl.ds(CONV_TS, CONV_HALO)

    @pl.when(i % tiles_per_seq == 0)
    def _():
        buf_ref[halo, :] = jnp.zeros((CONV_HALO, CONV_CH), F32)

    @pl.when(i % tiles_per_seq != 0)
    def _():
        buf_ref[halo, :] = buf_ref[tail, :]

    x = x_ref[...]
    r = lax.rsqrt(jnp.mean(x * x, axis=-1, keepdims=True) + EPS)
    xg = (x * g_ref[...]).astype(BF16)

    def proj(col0, width):
        return jnp.dot(xg, w_ref[:, col0:col0 + width], preferred_element_type=F32) * r

    bias = cb_ref[...]
    gain = cg_ref[...]
    beta = cbeta_ref[...]
    part = CONV_CH // CONV_PARTS
    groups = CONV_RC // SUBLANES

    def head_norm(z, gain_ref):
        groups = [slice(c * HEAD_GROUP, (c + 1) * HEAD_GROUP) for c in range(ATT_WIDTH // HEAD_GROUP)]
        return jnp.concatenate([_head_norm_group(z[:, s], gain_ref[:, s], e_ref[...]) for s in groups], axis=1)

    def project_q():
        q_ref[...] = head_norm(proj(2 * CONV_CH, ATT_WIDTH), qg_ref) * (HEAD_DIM ** -0.5)
        return q_ref

    def project_k():
        k_ref[...] = head_norm(proj(2 * CONV_CH + ATT_WIDTH, ATT_WIDTH), kg_ref)
        return k_ref

    def project_v():
        v_ref[...] = proj(2 * CONV_CH + 2 * ATT_WIDTH, ATT_WIDTH)
        return v_ref

    never = i < 0
    att_projections = [project_q, project_k, project_v]

    for n, lanes in enumerate(slice(n * part, (n + 1) * part) for n in range(CONV_PARTS)):
        glu = proj(lanes.start, part) * jax.nn.sigmoid(proj(CONV_CH + lanes.start, part))
        if n > 0 and att_projections:
            stored = att_projections.pop(0)()
            glu = glu + jnp.where(never, stored[0:1, 0:part], 0.0)
        buf_ref[CONV_HALO:, lanes] = glu
        for s in range(1, SUBLANES):
            sh_ref[s - 1, :, lanes] = buf_ref[s:s + CONV_SH_ROWS, lanes]
        for c in range(CONV_TS // CONV_RC):
            r0 = c * CONV_RC
            acc = jnp.broadcast_to(bias[:, lanes], (groups, SUBLANES, part))
            for t in range(CONV_WIDTH):
                off = CONV_PAD + t
                phase, base = off % SUBLANES, off - off % SUBLANES
                if phase == 0:
                    rows = buf_ref[r0 + base:r0 + base + CONV_RC, lanes]
                else:
                    rows = sh_ref[phase - 1, r0 + base:r0 + base + CONV_RC, lanes]
                acc = acc + rows.reshape(groups, SUBLANES, part) * cw_ref[t, :, lanes][None]
            pre_ref[r0:r0 + CONV_RC, lanes] = acc.reshape(CONV_RC, part)
    for project in att_projections:
        project()

    for r0 in range(0, CONV_TS, CONV_LN_ROWS):
        acc = pre_ref[r0:r0 + CONV_LN_ROWS, :]
        mu = jnp.mean(acc, axis=-1, keepdims=True)
        cen = acc - mu
        var = jnp.mean(cen * cen, axis=-1, keepdims=True)
        y = cen * lax.rsqrt(var + EPS) * gain + beta
        yc_ref[r0:r0 + CONV_LN_ROWS, :] = (y * jax.nn.sigmoid(y)).astype(yc_ref.dtype)


def _mixin(x, seq_len, g, w, qg, kg, e, cw, cb, cg, cbeta, wo):
    n = x.shape[0]
    d_in = w.shape[1]
    wo_slab = pl.BlockSpec((wo.shape[0] // (n // INP_TM), wo.shape[1]), lambda i: (i, 0))
    att = jax.ShapeDtypeStruct((n, ATT_WIDTH), F32)
    tile = lambda width: pl.BlockSpec((INP_TM, width), lambda i: (i, 0))
    const = lambda shape, **kw: pl.BlockSpec(shape, lambda i: (0,) * len(shape), **kw)
    return pl.pallas_call(
        functools.partial(_mixin_kernel, seq_len // INP_TM),
        grid=(n // INP_TM,),
        in_specs=[
            tile(D_MODEL),
            const((1, D_MODEL)),
            const((D_MODEL, d_in), pipeline_mode=pl.Buffered(1)),
            const((1, ATT_WIDTH)),
            const((1, ATT_WIDTH)),
            const((HEAD_GROUP, HEAD_GROUP)),
            const((CONV_WIDTH, SUBLANES, CONV_CH)),
            const((1, CONV_CH)), const((1, CONV_CH)), const((1, CONV_CH)),
            wo_slab,
        ],
        out_specs=[tile(CONV_CH), tile(ATT_WIDTH), tile(ATT_WIDTH), tile(ATT_WIDTH), wo_slab],
        out_shape=[jax.ShapeDtypeStruct((n, CONV_CH), BF16), att, att, att,
                   jax.ShapeDtypeStruct(wo.shape, BF16)],
        scratch_shapes=[
            pltpu.VMEM((CONV_TS + CONV_HALO, CONV_CH), F32),
            pltpu.VMEM((SUBLANES - 1, CONV_SH_ROWS, CONV_CH), F32),
            pltpu.VMEM((CONV_TS, CONV_CH), F32),
        ],
        compiler_params=pltpu.CompilerParams(
            dimension_semantics=("arbitrary",),
            vmem_limit_bytes=VMEM_LIMIT),
        name="mix_in",
    )(x, g, w, qg, kg, e, cw, cb, cg, cbeta, wo)


ATT_CH = 2048
ATT_QB = WIN
ATT_KB = 2 * WIN
ATT_TILES = ATT_CH // ATT_QB
ATT_STAGE = 4
ATT_QUART = ATT_CH // ATT_STAGE
ATT_SEG = ATT_QB // ATT_STAGE
ATT_UNROLL = 16


def _attn_kernel(slopes_ref, q_ref, kc_ref, vc_ref, wa_ref, wb_ref, wc_ref, o_ref, wa_out, wb_out, wc_out,
                 tmp, qd, kd, vd, biasbuf, xbuf, ybuf, mbuf, nat):
    hp = pl.program_id(1)
    c = pl.program_id(2)
    first_chunk = c == 0
    for src, dst in ((wa_ref, wa_out), (wb_ref, wb_out), (wc_ref, wc_out)):
        dst[...] = src[...].astype(BF16)

    lane = lax.broadcasted_iota(jnp.int32, (1, LANES), 1)
    in_a = lane < HEAD_DIM
    qi = lax.broadcasted_iota(jnp.int32, (ATT_QB, ATT_KB), 0)
    kj = lax.broadcasted_iota(jnp.int32, (ATT_QB, ATT_KB), 1)

    for ai, ref in enumerate((q_ref, kc_ref, vc_ref)):
        for a in range(ATT_STAGE):
            tmp[ai, a * ATT_QUART:(a + 1) * ATT_QUART, :] = ref[0, pl.ds(a, ATT_QUART, stride=ATT_STAGE), :]

    slot = {d: n for n, d in enumerate(d for d in DILATIONS if d != 1)}
    nwin_of = lambda d: ATT_CH // (d * WIN) + 1

    def grouped_rows(d, blk, res):
        if d == 1:
            return [(slice(a * ATT_SEG, (a + 1) * ATT_SEG),
                     pl.ds(pl.multiple_of(a * ATT_QUART + blk * ATT_SEG, ATT_SEG), ATT_SEG))
                    for a in range(ATT_STAGE)]
        if d == ATT_STAGE:
            return [(slice(None), pl.ds(pl.multiple_of(res * ATT_QUART + blk * ATT_QB, ATT_QB), ATT_QB))]
        hi = res // ATT_STAGE
        return [(slice(None), pl.ds((res - hi * ATT_STAGE) * ATT_QUART + hi, ATT_QB, stride=ATT_STAGE))]

    for d, sb in slot.items():
        nwin = nwin_of(d)

        def regroup_cur(t, carry, d=d, sb=sb, nwin=nwin):
            blk = t // d
            res = t - blk * d
            (_, idx), = grouped_rows(d, blk, res)
            qv, kv, vv = tmp[0, idx, :], tmp[1, idx, :], tmp[2, idx, :]
            q0 = pl.multiple_of(t * (2 * ATT_QB), 2 * ATT_QB)
            qd[sb, pl.ds(q0, ATT_QB), :] = jnp.where(in_a, qv, 0.0).astype(BF16)
            qd[sb, pl.ds(q0 + ATT_QB, ATT_QB), :] = jnp.where(in_a, 0.0, qv).astype(BF16)
            dst = pl.ds(pl.multiple_of((res * nwin + blk + 1) * ATT_QB, ATT_QB), ATT_QB)
            kd[sb, dst, :] = kv.astype(BF16)
            vd[sb, 0, dst, :] = jnp.where(in_a, vv, 1.0).astype(BF16)
            vd[sb, 1, dst, :] = jnp.where(in_a, 1.0, vv).astype(BF16)
            return carry

        def carry_prev(res, carry, sb=sb, nwin=nwin):
            dst = pl.ds(pl.multiple_of(res * nwin * ATT_QB, ATT_QB), ATT_QB)
            src = pl.ds(pl.multiple_of((res * nwin + nwin - 1) * ATT_QB, ATT_QB), ATT_QB)
            kd[sb, dst, :] = kd[sb, src, :]
            vd[sb, 0, dst, :] = vd[sb, 0, src, :]
            vd[sb, 1, dst, :] = vd[sb, 1, src, :]
            return carry

        def zero_prev(res, carry, sb=sb, nwin=nwin):
            dst = pl.ds(pl.multiple_of(res * nwin * ATT_QB, ATT_QB), ATT_QB)
            zeros = jnp.zeros((ATT_QB, LANES), BF16)
            kd[sb, dst, :] = zeros
            vd[sb, 0, dst, :] = zeros
            vd[sb, 1, dst, :] = zeros
            return carry

        @pl.when(first_chunk)
        def _(d=d, zero_prev=zero_prev):
            lax.fori_loop(0, d, zero_prev, 0)

        @pl.when(jnp.logical_not(first_chunk))
        def _(d=d, carry_prev=carry_prev):
            lax.fori_loop(0, d, carry_prev, 0)

        lax.fori_loop(0, ATT_TILES, regroup_cur, 0, unroll=2)

    for bi, d in enumerate(DILATIONS):
        def step_of(idx, d=d):
            if d == 1:
                return ATT_STAGE * (idx % ATT_SEG) + idx // ATT_SEG
            return idx

        dist = WIN + step_of(qi) - (step_of(kj % WIN) + WIN * (kj // WIN))
        valid = (dist >= 0) & (dist <= WIN)
        valid_first = valid & (kj >= WIN)
        distf = dist.astype(F32)
        for hh in range(2):
            alibi = (-slopes_ref[2 * hp + hh] * d) * distf
            rows = slice(hh * ATT_QB, (hh + 1) * ATT_QB)
            biasbuf[bi, 0, rows, :] = jnp.where(valid, alibi, NEG)
            biasbuf[bi, 1, rows, :] = jnp.where(valid_first, alibi, NEG)

        def operands(blk, res, t, d=d):
            if d != 1:
                sb, nwin = slot[d], nwin_of(d)
                q_rows = pl.ds(pl.multiple_of(t * (2 * ATT_QB), 2 * ATT_QB), 2 * ATT_QB)
                k_rows = pl.ds(pl.multiple_of((res * nwin + blk) * ATT_QB, ATT_QB), ATT_KB)
                return qd[sb, q_rows, :], kd[sb, k_rows, :], vd[sb, 0, k_rows, :], vd[sb, 1, k_rows, :]
            sb, nwin = slot[ATT_STAGE], nwin_of(ATT_STAGE)
            own = blk // ATT_STAGE
            part = blk - own * ATT_STAGE
            run = lambda start: pl.ds(pl.multiple_of(start, ATT_SEG), ATT_SEG)
            q_runs = [run(((own * ATT_STAGE + a) * 2 + hh) * ATT_QB + part * ATT_SEG)
                      for hh in range(2) for a in range(ATT_STAGE)]
            k_runs = [run((a * nwin + 1) * ATT_QB + (blk - 1 + w) * ATT_SEG)
                      for w in range(2) for a in range(ATT_STAGE)]
            gather = lambda ref_at, runs: jnp.concatenate([ref_at(r) for r in runs], axis=0)
            return (gather(lambda r: qd[sb, r, :], q_runs), gather(lambda r: kd[sb, r, :], k_runs),
                    gather(lambda r: vd[sb, 0, r, :], k_runs), gather(lambda r: vd[sb, 1, r, :], k_runs))

        def tile(t, carry, d=d, bi=bi, operands=operands):
            blk = t // d
            res = t - blk * d
            q2, k, va, vb = operands(blk, res, t)
            first = jnp.logical_and(first_chunk, blk == 0).astype(jnp.int32)
            s = lax.dot_general(q2, k, (((1,), (1,)), ((), ())), preferred_element_type=F32)
            s = s + biasbuf[bi, first]
            m = jnp.max(s, axis=-1, keepdims=True)
            p = jnp.exp(s - m).astype(BF16)
            oa = jnp.dot(p[:ATT_QB], va, preferred_element_type=F32)
            ob = jnp.dot(p[ATT_QB:], vb, preferred_element_type=F32)
            ma = jnp.broadcast_to(m[:ATT_QB], (ATT_QB, LANES))
            mb = jnp.broadcast_to(m[ATT_QB:], (ATT_QB, LANES))
            num = jnp.where(in_a, oa, ob)
            den = jnp.where(in_a, ob, oa)
            m_num = jnp.where(in_a, ma, mb)
            m_den = jnp.where(in_a, mb, ma)
            for rows, dst in grouped_rows(d, blk, res):
                xbuf[bi, dst, :] = num[rows]
                ybuf[bi, dst, :] = den[rows]
                mbuf[bi, 0, dst, :] = m_num[rows]
                mbuf[bi, 1, dst, :] = m_den[rows]
            return carry

        lax.fori_loop(0, ATT_TILES, tile, 0, unroll=ATT_UNROLL)

    def combine(t, carry):
        rows = pl.ds(pl.multiple_of(t * ATT_QB, ATT_QB), ATT_QB)

        def weighted_sum(which, vals):
            ms = [mbuf[bi, which, rows, :] for bi in range(len(DILATIONS))]
            mm = functools.reduce(jnp.maximum, ms)
            return sum(jnp.exp(mb - mm) * vals[bi, rows, :] for bi, mb in enumerate(ms))

        num = weighted_sum(0, xbuf)
        den = weighted_sum(1, ybuf)
        a = t // (ATT_QUART // ATT_QB)
        blk = t - a * (ATT_QUART // ATT_QB)
        nat[pl.ds(blk * (ATT_STAGE * ATT_QB) + a, ATT_QB, stride=ATT_STAGE), :] = (
            num / pltpu.roll(den, HEAD_DIM, axis=1))
        return carry

    lax.fori_loop(0, ATT_TILES, combine, 0, unroll=2)
    o_ref[0] = nat[...].astype(o_ref.dtype)


def _attn(q, k, v, slopes, wa, wb, wc):
    bsz, s, _ = q.shape
    blk = (1, ATT_CH, LANES)
    nb = len(DILATIONS)
    grid = (bsz, ATT_WIDTH // LANES, s // ATT_CH)
    steps = grid[0] * grid[1] * grid[2]
    cur = pl.BlockSpec(blk, lambda b, h, c, sl: (b, c, h))

    def slab(w):
        rows, cols = w.shape
        count = steps
        while rows % count or (rows // count) % BF16_SUBLANES:
            count //= 2
        per = steps // count
        return pl.BlockSpec((rows // count, cols),
                            lambda b, h, c, sl: (((b * grid[1] + h) * grid[2] + c) // per, 0))

    slabs = [slab(w) for w in (wa, wb, wc)]
    grid_spec = pltpu.PrefetchScalarGridSpec(
        num_scalar_prefetch=1,
        grid=grid,
        in_specs=[cur, cur, cur] + slabs,
        out_specs=[cur] + slabs,
        scratch_shapes=[
            pltpu.VMEM((3, ATT_CH, LANES), F32),
            pltpu.VMEM((nb - 1, 2 * ATT_CH, LANES), BF16),
            pltpu.VMEM((nb - 1, 2 * ATT_CH, LANES), BF16),
            pltpu.VMEM((nb - 1, 2, 2 * ATT_CH, LANES), BF16),
            pltpu.VMEM((nb, 2, 2 * ATT_QB, ATT_KB), F32),
            pltpu.VMEM((nb, ATT_CH, LANES), F32),
            pltpu.VMEM((nb, ATT_CH, LANES), F32),
            pltpu.VMEM((nb, 2, ATT_CH, LANES), F32),
            pltpu.VMEM((ATT_CH, LANES), F32),
        ],
    )
    return pl.pallas_call(
        _attn_kernel,
        grid_spec=grid_spec,
        out_shape=[jax.ShapeDtypeStruct((bsz, s, ATT_WIDTH), BF16)]
        + [jax.ShapeDtypeStruct(w.shape, BF16) for w in (wa, wb, wc)],
        compiler_params=pltpu.CompilerParams(
            dimension_semantics=("parallel", "arbitrary", "arbitrary"),
            vmem_limit_bytes=VMEM_LIMIT),
        name="attn",
    )(slopes, q, k, v, wa, wb, wc)


OUT_TM = 512


def _outproj_kernel(x_ref, yc_ref, ya_ref, wc_ref, wa_ref, o_ref):
    o_ref[...] = (x_ref[...]
                  + jnp.dot(yc_ref[...], wc_ref[...], preferred_element_type=F32)
                  + jnp.dot(ya_ref[...], wa_ref[...], preferred_element_type=F32))


def _outproj(x, yc, ya, w):
    n = x.shape[0]
    assert CONV_CH == ATT_WIDTH
    return pl.pallas_call(
        _outproj_kernel,
        grid=(n // OUT_TM,),
        in_specs=[
            pl.BlockSpec((OUT_TM, D_MODEL), lambda i: (i, 0)),
            pl.BlockSpec((OUT_TM, CONV_CH), lambda i: (i, 0)),
            pl.BlockSpec((OUT_TM, ATT_WIDTH), lambda i: (i, 0)),
            pl.BlockSpec((CONV_CH, D_MODEL), lambda i: (0, 0)),
            pl.BlockSpec((ATT_WIDTH, D_MODEL), lambda i: (1, 0)),
        ],
        out_specs=pl.BlockSpec((OUT_TM, D_MODEL), lambda i: (i, 0)),
        out_shape=jax.ShapeDtypeStruct((n, D_MODEL), F32),
        compiler_params=pltpu.CompilerParams(
            dimension_semantics=("parallel",),
            vmem_limit_bytes=VMEM_LIMIT),
        name="out_proj",
    )(x, yc, ya, w, w)


def _layer(x2d, bsz, s, p):
    n = bsz * s
    x2d = _ffn(x2d, p["ffn1_norm_g"], p["ffn1_w_gate"], p["ffn1_w_up"], p["ffn1_w_down"])
    yc, q, k, v, w_out = _mixin(x2d, s, p["mix_norm_g"], p["w_in"], p["q_norm_g"], p["k_norm_g"],
                                p["head_ones"], p["conv_w_dw"], p["conv_b_dw"], p["conv_ln_g"],
                                p["conv_ln_b"], p["w_out"])
    ya, w_gate2, w_up2, w_down2 = _attn(
        q.reshape(bsz, s, ATT_WIDTH), k.reshape(bsz, s, ATT_WIDTH), v.reshape(bsz, s, ATT_WIDTH),
        p["slopes"], p["ffn2_w_gate"], p["ffn2_w_up"], p["ffn2_w_down"])
    x2d = _outproj(x2d, yc.reshape(n, CONV_CH), ya.reshape(n, ATT_WIDTH), w_out)
    return _ffn(x2d, p["ffn2_norm_g"], w_gate2, w_up2, w_down2)


def kernel(x, ffn1_norm_g, ffn1_w_gate, ffn1_w_up, ffn1_w_down, mix_norm_g, w_in, conv_w_dw, conv_b_dw,
           conv_ln_g, conv_ln_b, q_norm_g, k_norm_g, w_out, ffn2_norm_g, ffn2_w_gate, ffn2_w_up, ffn2_w_down):
    bsz, s, _ = x.shape
    depth = w_in.shape[0]
    row = lambda a: a.reshape(1, -1).astype(F32)
    head_ids = jnp.arange(HEAD_GROUP) // HEAD_DIM
    head_ones = (head_ids[:, None] == head_ids[None, :]).astype(BF16)
    slopes = 2.0 ** (-ALIBI_MAX_BIAS * jnp.arange(1, N_HEADS + 1, dtype=F32) / N_HEADS)
    x2d = x.reshape(bsz * s, D_MODEL)
    for l in range(depth):
        p = dict(
            ffn1_norm_g=row(ffn1_norm_g[l]), ffn1_w_gate=ffn1_w_gate[l].astype(BF16),
            ffn1_w_up=ffn1_w_up[l].astype(BF16), ffn1_w_down=ffn1_w_down[l].astype(BF16),
            mix_norm_g=row(mix_norm_g[l]), w_in=w_in[l].astype(BF16),
            q_norm_g=row(jnp.tile(q_norm_g[l], N_HEADS)), k_norm_g=row(jnp.tile(k_norm_g[l], N_HEADS)),
            head_ones=head_ones, slopes=slopes,
            conv_w_dw=jnp.broadcast_to(conv_w_dw[l].astype(F32)[:, None, :], (CONV_WIDTH, SUBLANES, CONV_CH)),
            conv_b_dw=row(conv_b_dw[l]),
            conv_ln_g=row(conv_ln_g[l]), conv_ln_b=row(conv_ln_b[l]),
            w_out=w_out[l].astype(F32),
            ffn2_norm_g=row(ffn2_norm_g[l]), ffn2_w_gate=ffn2_w_gate[l].astype(F32),
            ffn2_w_up=ffn2_w_up[l].astype(F32), ffn2_w_down=ffn2_w_down[l].astype(F32),
        )
        x2d = _layer(x2d, bsz, s, p)
    return x2d.reshape(bsz, s, D_MODEL)
```
